```python
import jax, jax.numpy as jnp
from jax import lax
import numpy as np

D_MODEL = 4096
BATCH = 1
SEQ = 8192
DEPTH = 4

N_A_LAYERS = DEPTH // 2
N_B_LAYERS = DEPTH - N_A_LAYERS
RWKV_HEAD = 64
RWKV_HEADS = D_MODEL // RWKV_HEAD
DECAY_LORA = max(32, int(round(1.8 * D_MODEL ** 0.5 / 32)) * 32)
ICLR_LORA = max(32, int(round(1.8 * D_MODEL ** 0.5 / 32)) * 32)
VRES_LORA = max(32, int(round(1.3 * D_MODEL ** 0.5 / 32)) * 32)
GATE_LORA = max(32, int(round(0.6 * D_MODEL ** 0.8 / 32)) * 32)
LNX_EPS = 64e-5
SB_HEADS = 32
SB_HEAD_DIM = D_MODEL // SB_HEADS
SB_BLOCK = 128
D_FF = 256 * ((8 * D_MODEL // 3 + 255) // 256)
CONV_WIDTH = 3
RMS_EPS = 1e-6

kernel_name = "yoco_rwkv7_stickbreaking_convglu"


def rms_norm(x, w):
    xf = x.astype(jnp.float32)
    y = xf * lax.rsqrt(jnp.mean(xf * xf, axis=-1, keepdims=True) + RMS_EPS)
    return (y * w.astype(jnp.float32)).astype(x.dtype)


def token_shift(x):
    return jnp.pad(x[:, :-1], ((0, 0), (1, 0), (0, 0)))


def wkv7_scan(r, w, k, v, a, b):
    bsz, _, nh, n = r.shape
    seq_major = [jnp.swapaxes(t, 0, 1) for t in (r, w, k, v, a, b)]

    def step(state, inp):
        r_t, w_t, k_t, v_t, a_t, b_t = inp
        sa = jnp.einsum('bhij,bhj->bhi', state, a_t)
        state = (state * w_t[:, :, None, :]
                 + sa[..., None] * b_t[:, :, None, :]
                 + v_t[..., None] * k_t[:, :, None, :])
        y_t = jnp.einsum('bhij,bhj->bhi', state, r_t)
        return state, y_t

    s0 = jnp.zeros((bsz, nh, n, n), jnp.float32)
    _, ys = lax.scan(step, s0, tuple(seq_major))
    return jnp.swapaxes(ys, 0, 1)


def rwkv7_time_mix(x, mix, w_r, w_k, w_v, w_o, w0, w1, w2, a0, a1, a2, g1, g2,
                   k_k, k_a, r_k, lnx_w, lnx_b, v_first, vres):
    bsz, seqlen, d = x.shape
    f32 = jnp.float32

    def heads(t):
        return t.reshape(bsz, seqlen, RWKV_HEADS, RWKV_HEAD)

    xx = token_shift(x) - x
    xr = x + xx * mix[0]
    xw = x + xx * mix[1]
    xk = x + xx * mix[2]
    xv = x + xx * mix[3]
    xa = x + xx * mix[4]
    xg = x + xx * mix[5]

    r = xr @ w_r
    k = xk @ w_k
    v = xv @ w_v
    w_log = -jax.nn.softplus(-(w0 + jnp.tanh(xw @ w1) @ w2).astype(f32)) - 0.5
    decay = jnp.exp(-jnp.exp(w_log))
    if vres is None:
        v_first = v
    else:
        v0, v1, v2 = vres
        v = v + (v_first - v) * jax.nn.sigmoid(v0 + (xv @ v1) @ v2)
    a = jax.nn.sigmoid(a0 + (xa @ a1) @ a2)
    g = jax.nn.sigmoid(xg @ g1) @ g2

    kk = heads(k * k_k).astype(f32)
    kk = kk / jnp.maximum(jnp.sqrt(jnp.sum(kk * kk, axis=-1, keepdims=True)), 1e-12)
    k = k * (1.0 + (a - 1.0) * k_a)

    rh, kh, vh = heads(r).astype(f32), heads(k).astype(f32), heads(v).astype(f32)
    y = wkv7_scan(rh, heads(decay), kh, vh, -kk, kk * heads(a).astype(f32))

    mu = jnp.mean(y, axis=-1, keepdims=True)
    var = jnp.mean(jnp.square(y - mu), axis=-1, keepdims=True)
    y = ((y - mu) * lax.rsqrt(var + LNX_EPS)).reshape(bsz, seqlen, d)
    y = y * lnx_w.astype(f32) + lnx_b.astype(f32)
    rk = r_k.reshape(RWKV_HEADS, RWKV_HEAD).astype(f32)
    bonus = (jnp.sum(rh * kh * rk, axis=-1, keepdims=True) * vh).reshape(bsz, seqlen, d)
    out = ((y + bonus).astype(x.dtype) * g) @ w_o
    return out, v_first


def split_heads(t):
    bsz, seqlen, _ = t.shape
    return t.reshape(bsz, seqlen, SB_HEADS, SB_HEAD_DIM).transpose(0, 2, 1, 3)


def merge_heads(t):
    bsz, _, seqlen, _ = t.shape
    return t.transpose(0, 2, 1, 3).reshape(bsz, seqlen, SB_HEADS * SB_HEAD_DIM)


def stick_breaking_attention(q, k, v):
    seqlen = q.shape[2]
    scale = 1.0 / (SB_HEAD_DIM ** 0.5)
    outs = []
    for t0 in range(0, seqlen, SB_BLOCK):
        t1 = t0 + SB_BLOCK
        qb = q[:, :, t0:t1]
        kb = k[:, :, :t1]
        vb = v[:, :, :t1]
        z = jnp.einsum('bhqd,bhkd->bhqk', qb, kb).astype(jnp.float32) * scale
        qpos = t0 + jnp.arange(SB_BLOCK)[:, None]
        kpos = jnp.arange(t1)[None, :]
        strict = kpos < qpos
        log_one_minus_beta = jnp.where(strict, jax.nn.log_sigmoid(-z), 0.0)
        after = lax.cumsum(log_one_minus_beta, axis=3, reverse=True) - log_one_minus_beta
        attn = jnp.where(strict, jnp.exp(jax.nn.log_sigmoid(z) + after), 0.0)
        outs.append(jnp.einsum('bhqk,bhkd->bhqd', attn.astype(vb.dtype), vb))
    return jnp.concatenate(outs, axis=2)


def conv_glu_ffn(x, w_up, conv_w, conv_b, w_down):
    seqlen = x.shape[1]
    gate, up = jnp.split(x @ w_up, 2, axis=-1)
    gp = jnp.pad(gate, ((0, 0), (CONV_WIDTH - 1, 0), (0, 0)))
    conv = conv_b + gp[:, 0:seqlen] * conv_w[0]
    for i in range(1, CONV_WIDTH):
        conv = conv + gp[:, i:i + seqlen] * conv_w[i]
    return (jax.nn.silu(conv) * up) @ w_down


def setup_inputs(seed: int = 0) -> dict:
    key = jax.random.key(seed)
    keys = iter(jax.random.split(key, 64))
    f32 = jnp.float32
    D, F, NA, NB = D_MODEL, D_FF, N_A_LAYERS, N_B_LAYERS

    def nrm(shape, scale):
        return jax.random.normal(next(keys), shape, f32) * scale

    def gain(shape):
        return 1.0 + 0.02 * jax.random.normal(next(keys), shape, f32)

    def unif(shape, lo, hi):
        return jax.random.uniform(next(keys), shape, f32, minval=lo, maxval=hi)

    nv = max(NA - 1, 0)
    return {
        "x": nrm((BATCH, SEQ, D), 1.0),
        "ln_mix_w": gain((DEPTH, D)),
        "ln_ffn_w": gain((DEPTH, D)),
        "rwkv_mix": unif((NA, 6, D), 0.0, 1.0),
        "rwkv_w_r": nrm((NA, D, D), D ** -0.5),
        "rwkv_w_k": nrm((NA, D, D), D ** -0.5),
        "rwkv_w_v": nrm((NA, D, D), D ** -0.5),
        "rwkv_w_o": nrm((NA, D, D), D ** -0.5),
        "rwkv_decay_w0": unif((NA, D), -6.0, 0.0),
        "rwkv_decay_w1": nrm((NA, D, DECAY_LORA), D ** -0.5),
        "rwkv_decay_w2": nrm((NA, DECAY_LORA, D), 0.1 * DECAY_LORA ** -0.5),
        "rwkv_iclr_a0": nrm((NA, D), 0.1),
        "rwkv_iclr_a1": nrm((NA, D, ICLR_LORA), D ** -0.5),
        "rwkv_iclr_a2": nrm((NA, ICLR_LORA, D), ICLR_LORA ** -0.5),
        "rwkv_gate_g1": nrm((NA, D, GATE_LORA), D ** -0.5),
        "rwkv_gate_g2": nrm((NA, GATE_LORA, D), GATE_LORA ** -0.5),
        "rwkv_k_k": 0.85 + 0.02 * jax.random.normal(next(keys), (NA, D), f32),
        "rwkv_k_a": gain((NA, D)),
        "rwkv_r_k": nrm((NA, D), 0.1),
        "rwkv_lnx_w": gain((NA, D)),
        "rwkv_lnx_b": nrm((NA, D), 0.02),
        "vres_v0": 1.0 + 0.1 * jax.random.normal(next(keys), (nv, D), f32),
        "vres_v1": nrm((nv, D, VRES_LORA), D ** -0.5),
        "vres_v2": nrm((nv, VRES_LORA, D), VRES_LORA ** -0.5),
        "kv_norm_w": gain((D,)),
        "sb_w_k": nrm((D, D), D ** -0.5),
        "sb_w_v": nrm((D, D), D ** -0.5),
        "sb_w_q": nrm((NB, D, D), D ** -0.5),
        "sb_w_o": nrm((NB, D, D), D ** -0.5),
        "ffn_w_up": nrm((DEPTH, D, 2 * F), D ** -0.5),
        "ffn_conv_w": nrm((DEPTH, CONV_WIDTH, F), CONV_WIDTH ** -0.5),
        "ffn_conv_b": nrm((DEPTH, F), 0.02),
        "ffn_w_down": nrm((DEPTH, F, D), F ** -0.5),
        "final_norm_w": gain((D,)),
    }


def reference(x, ln_mix_w, ln_ffn_w, rwkv_mix, rwkv_w_r, rwkv_w_k, rwkv_w_v, rwkv_w_o,
              rwkv_decay_w0, rwkv_decay_w1, rwkv_decay_w2, rwkv_iclr_a0, rwkv_iclr_a1,
              rwkv_iclr_a2, rwkv_gate_g1, rwkv_gate_g2, rwkv_k_k, rwkv_k_a, rwkv_r_k,
              rwkv_lnx_w, rwkv_lnx_b, vres_v0, vres_v1, vres_v2, kv_norm_w, sb_w_k, sb_w_v,
              sb_w_q, sb_w_o, ffn_w_up, ffn_conv_w, ffn_conv_b, ffn_w_down, final_norm_w):
    h = x
    v_first = None
    shared_k = None
    shared_v = None
    for layer in range(DEPTH):
        hn = rms_norm(h, ln_mix_w[layer])
        if layer < N_A_LAYERS:
            i = layer
            vres = None if i == 0 else (vres_v0[i - 1], vres_v1[i - 1], vres_v2[i - 1])
            mixed, v_first = rwkv7_time_mix(
                hn, rwkv_mix[i], rwkv_w_r[i], rwkv_w_k[i], rwkv_w_v[i], rwkv_w_o[i],
                rwkv_decay_w0[i], rwkv_decay_w1[i], rwkv_decay_w2[i],
                rwkv_iclr_a0[i], rwkv_iclr_a1[i], rwkv_iclr_a2[i],
                rwkv_gate_g1[i], rwkv_gate_g2[i], rwkv_k_k[i], rwkv_k_a[i], rwkv_r_k[i],
                rwkv_lnx_w[i], rwkv_lnx_b[i], v_first, vres)
        else:
            if layer == N_A_LAYERS:
                kvn = rms_norm(h, kv_norm_w)
                shared_k = split_heads(kvn @ sb_w_k)
                shared_v = split_heads(kvn @ sb_w_v)
            j = layer - N_A_LAYERS
            q = split_heads(hn @ sb_w_q[j])
            mixed = merge_heads(stick_breaking_attention(q, shared_k, shared_v)) @ sb_w_o[j]
        h = h + mixed
        h = h + conv_glu_ffn(rms_norm(h, ln_ffn_w[layer]), ffn_w_up[layer],
                             ffn_conv_w[layer], ffn_conv_b[layer], ffn_w_down[layer])
    return rms_norm(h, final_norm_w)
```

```python
import functools

import jax
import jax.numpy as jnp
from jax import lax
from jax.experimental import pallas as pl
from jax.experimental.pallas import tpu as pltpu

F32 = jnp.float32
BF16 = jnp.bfloat16

RWKV_HEAD = 64
LNX_EPS = 64e-5
SB_HEAD_DIM = 128
SB_BLOCK = 128
CONV_WIDTH = 3
RMS_EPS = 1e-6

LANES = 128
SUBLANES = 8
GLU_HALO = 2 * SUBLANES
VMEM_LIMIT_BYTES = 56 * 1024 * 1024

WKV_CHUNK = LANES // 2
SB_DEAD_LOG = -110.0


def _cparams(*sem):
    return pltpu.CompilerParams(dimension_semantics=sem, vmem_limit_bytes=VMEM_LIMIT_BYTES)


def _pick(n, prefs):
    for p in prefs:
        if n % p == 0:
            return p
    return n


def _dot(a, b, dims=(((1,), (0,)), ((), ())), precision=None):
    return lax.dot_general(a, b, dims, precision=precision, preferred_element_type=F32)


_NT = (((1,), (1,)), ((), ()))
_TN = (((0,), (0,)), ((), ()))
_HI = lax.Precision.HIGHEST


def _rms(x, w):
    return x * lax.rsqrt(jnp.mean(x * x, axis=-1, keepdims=True) + RMS_EPS) * w


def _rmsnorm_kernel(x_ref, w_ref, o_ref):
    o_ref[...] = _rms(x_ref[...], w_ref[...]).astype(o_ref.dtype)


def _rmsnorm(x, w, out_dtype):
    t, d = x.shape
    bt = _pick(t, (256, 128, 64, 32, 16))
    return pl.pallas_call(
        _rmsnorm_kernel,
        out_shape=jax.ShapeDtypeStruct((t, d), out_dtype),
        grid=(t // bt,),
        in_specs=[pl.BlockSpec((bt, d), lambda i: (i, 0)),
                  pl.BlockSpec((1, d), lambda i: (0, 0))],
        out_specs=pl.BlockSpec((bt, d), lambda i: (i, 0)),
        compiler_params=_cparams("parallel"),
        name="rmsnorm",
    )(x, w.reshape(1, d))


def _norm_mix_kernel(x_ref, xp_ref, w_ref, mix_ref, *o_refs):
    i = pl.program_id(0)
    w = w_ref[...]
    hn = _rms(x_ref[...], w)
    prev = _rms(xp_ref[SUBLANES - 1:SUBLANES, :], w) * (i > 0).astype(F32)
    row = lax.broadcasted_iota(jnp.int32, hn.shape, 0)
    shifted = jnp.where(row == 0, prev, pltpu.roll(hn, 1, 0))
    xx = shifted - hn
    for j, o_ref in enumerate(o_refs):
        o_ref[...] = (hn + xx * mix_ref[j:j + 1, :]).astype(o_ref.dtype)


def _norm_mix(x, w, mix):
    t, d = x.shape
    bt = _pick(t, (256, 128, 64, 32, 16))
    per = bt // SUBLANES
    n = mix.shape[0]
    return pl.pallas_call(
        _norm_mix_kernel,
        out_shape=[jax.ShapeDtypeStruct((t, d), BF16)] * n,
        grid=(t // bt,),
        in_specs=[pl.BlockSpec((bt, d), lambda i: (i, 0)),
                  pl.BlockSpec((SUBLANES, d), lambda i: (jnp.maximum(i * per - 1, 0), 0)),
                  pl.BlockSpec((1, d), lambda i: (0, 0)),
                  pl.BlockSpec((n, d), lambda i: (0, 0))],
        out_specs=[pl.BlockSpec((bt, d), lambda i: (i, 0))] * n,
        compiler_params=_cparams("parallel"),
        name="norm_mix",
    )(x, x, w.reshape(1, d), mix)


def _mm_kernel(*refs, nk, has_res):
    a_ref, w_ref = refs[0], refs[1]
    res_ref = refs[2] if has_res else None
    o_ref = refs[2 + has_res]
    acc_ref = refs[3 + has_res] if nk > 1 else None
    part = _dot(a_ref[...].astype(BF16), w_ref[...].astype(BF16))

    def finish(acc):
        if has_res:
            acc = acc + res_ref[...]
        o_ref[...] = acc.astype(o_ref.dtype)

    if nk == 1:
        finish(part)
    else:
        kk = pl.program_id(2)

        @pl.when(kk == 0)
        def _():
            acc_ref[...] = part

        @pl.when(jnp.logical_and(kk > 0, kk < nk - 1))
        def _():
            acc_ref[...] += part

        @pl.when(kk == nk - 1)
        def _():
            finish(acc_ref[...] + part)


def _matmul(a, w, layer, *, out_dtype, res=None, bm=None, bn=None, bk=None, col0=0, ncols=None):
    m, k = a.shape
    n = w.shape[2] if ncols is None else ncols
    bm = bm or _pick(m, (1024, 512, 256, 128, 64, 32, 16))
    bn = bn or _pick(n, (512, 256, 128))
    bk = bk or k
    nk = k // bk
    assert m % bm == 0 and n % bn == 0 and k % bk == 0 and col0 % bn == 0
    cb = col0 // bn
    in_specs = [pl.BlockSpec((bm, bk), lambda i, j, q: (i, q)),
                pl.BlockSpec((None, bk, bn), lambda i, j, q: (layer, q, j + cb))]
    args = [a, w]
    aliases = {}
    if res is not None:
        in_specs.append(pl.BlockSpec((bm, bn), lambda i, j, q: (i, j)))
        args.append(res)
        aliases = {2: 0}
    return pl.pallas_call(
        functools.partial(_mm_kernel, nk=nk, has_res=res is not None),
        out_shape=jax.ShapeDtypeStruct((m, n), out_dtype),
        grid=(m // bm, n // bn, nk),
        in_specs=in_specs,
        out_specs=pl.BlockSpec((bm, bn), lambda i, j, q: (i, j)),
        scratch_shapes=[pltpu.VMEM((bm, bn), F32)] if nk > 1 else [],
        input_output_aliases=aliases,
        compiler_params=_cparams("parallel", "parallel", "arbitrary"),
        name="matmul",
    )(*args)


def _lora_kernel(x_ref, w1_ref, w2_ref, b_ref, o_ref, *, mid, post):
    hmid = _dot(x_ref[...], w1_ref[...].astype(BF16))
    if mid == "tanh":
        hmid = jnp.tanh(hmid)
    elif mid == "sigmoid":
        hmid = jax.nn.sigmoid(hmid)
    out = _dot(hmid.astype(BF16), w2_ref[...].astype(BF16)) + b_ref[...]
    if post == "sigmoid":
        out = jax.nn.sigmoid(out)
    o_ref[...] = out.astype(o_ref.dtype)


def _lora(x, w1, w2, bias, *, mid=None, post=None):
    t, d = x.shape
    r = w1.shape[1]
    rp = -(-r // LANES) * LANES
    if rp != r:
        w1 = jnp.pad(w1, ((0, 0), (0, rp - r)))
        w2 = jnp.pad(w2, ((0, rp - r), (0, 0)))
    bt = _pick(t, (512, 256, 128, 64, 32, 16))
    return pl.pallas_call(
        functools.partial(_lora_kernel, mid=mid, post=post),
        out_shape=jax.ShapeDtypeStruct((t, d), F32),
        grid=(t // bt,),
        in_specs=[pl.BlockSpec((bt, d), lambda i: (i, 0)),
                  pl.BlockSpec((d, rp), lambda i: (0, 0)),
                  pl.BlockSpec((rp, d), lambda i: (0, 0)),
                  pl.BlockSpec((1, d), lambda i: (0, 0))],
        out_specs=pl.BlockSpec((bt, d), lambda i: (i, 0)),
        compiler_params=_cparams("parallel"),
        name="lora",
    )(x, w1, w2, bias.reshape(1, d))


def _split3(x):
    hi = x.astype(BF16)
    r1 = x - hi.astype(F32)
    mid = r1.astype(BF16)
    lo = (r1 - mid.astype(F32)).astype(BF16)
    return hi, mid, lo


def _dot_exact_rhs(x, m_bf16):
    hi, mid, lo = _split3(x)
    return _dot(hi, m_bf16) + _dot(mid, m_bf16) + _dot(lo, m_bf16)


def _softplus(x):
    return jnp.maximum(x, 0.0) + jnp.log1p(jnp.exp(-jnp.abs(x)))


def _wkv_kernel(*refs, pairs, mix_v):
    if mix_v:
        (r_ref, k_ref, v_ref, wr_ref, a_ref, g_ref, vf_ref, vg_ref,
         kk_ref, ka_ref, rk_ref, lnw_ref, lnb_ref, o_ref, state_ref) = refs
    else:
        (r_ref, k_ref, v_ref, wr_ref, a_ref, g_ref,
         kk_ref, ka_ref, rk_ref, lnw_ref, lnb_ref, o_ref, state_ref) = refs
    c = WKV_CHUNK
    n2 = 2 * c

    @pl.when(pl.program_id(1) == 0)
    def _():
        state_ref[...] = jnp.zeros_like(state_ref)

    row = lax.broadcasted_iota(jnp.int32, (n2, n2), 0)
    col = lax.broadcasted_iota(jnp.int32, (n2, n2), 1)
    same = (row < c) == (col < c)
    strict = jnp.logical_and(same, col < row)
    incl = jnp.logical_and(same, col <= row)
    m0 = (lax.broadcasted_iota(jnp.int32, (c, n2), 1) < RWKV_HEAD).astype(F32)
    m1 = 1.0 - m0
    blk_ones = same.astype(BF16)
    tri = (lax.broadcasted_iota(jnp.int32, (c, c), 0)
           >= lax.broadcasted_iota(jnp.int32, (c, c), 1)).astype(F32)
    eye = (row == col).astype(F32)

    def stack(x):
        return jnp.concatenate([x * m0, x * m1], axis=0)

    def unstack(xs):
        return xs[:c] + xs[c:]

    for p in range(pairs):
        sl = slice(p * LANES, (p + 1) * LANES)
        r = r_ref[:, sl]
        k = k_ref[:, sl]
        v = v_ref[:, sl]
        a = a_ref[:, sl]
        if mix_v:
            v = v + (vf_ref[:, sl] - v) * vg_ref[:, sl]
        lw = -jnp.exp(-_softplus(-wr_ref[:, sl]) - 0.5)
        kraw = k * kk_ref[:, sl]
        ss = _dot_exact_rhs(kraw * kraw, blk_ones)
        kn = kraw / jnp.maximum(jnp.sqrt(ss), 1e-12)
        k2 = k * (1.0 + (a - 1.0) * ka_ref[:, sl])
        bvec = kn * a

        cum = _dot(tri, lw, precision=_HI)
        g_in = jnp.exp(cum)
        g_ex = jnp.exp(cum - lw)
        g_inv = jnp.exp(-cum)
        g_end = g_in[c - 1:c, :]

        ah = stack(-kn * g_ex)
        rh = stack(r * g_in)
        bh = stack(bvec * g_inv)
        kh = stack(k2 * g_inv)
        vs = stack(v)

        a_ab = jnp.where(strict, _dot(ah, bh, _NT, _HI), 0.0)
        a_ak = jnp.where(strict, _dot(ah, kh, _NT, _HI), 0.0)
        a_rb = jnp.where(incl, _dot(rh, bh, _NT, _HI), 0.0)
        a_rk = jnp.where(incl, _dot(rh, kh, _NT, _HI), 0.0)

        tinv = eye + a_ab
        pw = a_ab
        steps = (c - 1).bit_length() - 1
        for _ in range(steps):
            pw = _dot(pw, pw, precision=_HI)
            tinv = tinv + _dot(pw, tinv, precision=_HI)

        s_t = state_ref[p]
        x = _dot(ah, s_t, _NT, _HI) + _dot(a_ak, vs, precision=_HI)
        us = _dot(tinv, x, precision=_HI)
        ys = _dot(rh, s_t, _NT, _HI) + _dot(a_rb, us, precision=_HI) + _dot(a_rk, vs, precision=_HI)
        upd = _dot(us, bh * g_end, _TN, _HI) + _dot(vs, kh * g_end, _TN, _HI)
        state_ref[p] = jnp.where(same, s_t * g_end + upd, 0.0)
        y = unstack(ys)

        inv_n = 1.0 / RWKV_HEAD
        mu = _dot_exact_rhs(y, blk_ones) * inv_n
        dlt = y - mu
        var = _dot_exact_rhs(dlt * dlt, blk_ones) * inv_n
        yn = dlt * lax.rsqrt(var + LNX_EPS) * lnw_ref[:, sl] + lnb_ref[:, sl]
        bonus = _dot_exact_rhs(r * k2 * rk_ref[:, sl], blk_ones) * v
        o_ref[:, sl] = ((yn + bonus) * g_ref[:, sl]).astype(o_ref.dtype)


def _wkv(r, k, v, wraw, a, g, vfirst, vgate, k_k, k_a, r_k, lnx_w, lnx_b, *, pairs=None):
    t, d = r.shape
    c = WKV_CHUNK
    npairs = d // LANES
    pairs = pairs or _pick(npairs, (4, 2, 1))
    wd = pairs * LANES
    mix_v = vfirst is not None
    seqs = [r, k, v, wraw, a, g] + ([vfirst, vgate] if mix_v else [])
    seq = pl.BlockSpec((c, wd), lambda h, i: (i, h))
    par = pl.BlockSpec((1, wd), lambda h, i: (0, h))
    prm = [x.reshape(1, d) for x in (k_k, k_a, r_k, lnx_w, lnx_b)]
    return pl.pallas_call(
        functools.partial(_wkv_kernel, pairs=pairs, mix_v=mix_v),
        out_shape=jax.ShapeDtypeStruct((t, d), BF16),
        grid=(npairs // pairs, t // c),
        in_specs=[seq] * len(seqs) + [par] * len(prm),
        out_specs=seq,
        scratch_shapes=[pltpu.VMEM((pairs, LANES, LANES), F32)],
        compiler_params=_cparams("parallel", "arbitrary"),
        name="wkv7",
    )(*seqs, *prm)


def _sb_kernel(q_ref, k_ref, v_ref, o_ref):
    blk = SB_BLOCK
    qb = pl.program_id(1)
    q = q_ref[...]
    scale = 1.0 / (SB_HEAD_DIM ** 0.5)
    row = lax.broadcasted_iota(jnp.int32, (blk, blk), 0)
    col = lax.broadcasted_iota(jnp.int32, (blk, blk), 1)
    tri = jnp.concatenate([(row > col).astype(BF16), jnp.ones((blk, blk), BF16)], axis=1)

    def cond(st):
        kb, _, _, live = st
        return jnp.logical_and(kb >= 0, live > 0)

    def body(st):
        kb, acc, carry, _ = st
        off = pl.multiple_of(kb * blk, blk)
        kt = k_ref[pl.ds(off, blk), :]
        vt = v_ref[pl.ds(off, blk), :]
        z = _dot(q, kt, _NT) * scale
        strict = (col + kb * blk) < (row + qb * blk)
        sp = _softplus(z)
        l1m = jnp.where(strict, -sp, 0.0)
        hi, mid, lo = _split3(l1m)
        cs = _dot(hi, tri) + _dot(mid, tri) + _dot(lo, tri)
        logw = (z - sp) + cs[:, :blk] + carry
        attn = jnp.where(strict, jnp.exp(logw), 0.0)
        acc = acc + _dot(attn.astype(BF16), vt)
        carry = carry + cs[:, blk:]
        live = (jnp.max(carry) > SB_DEAD_LOG).astype(jnp.int32)
        return kb - 1, acc, carry, live

    zeros = jnp.zeros((blk, blk), F32)
    _, acc, _, _ = lax.while_loop(cond, body, (qb, zeros, zeros, jnp.int32(1)))
    o_ref[...] = acc.astype(o_ref.dtype)


def _sb_attention(q, k, v):
    t, d = q.shape
    nh = d // SB_HEAD_DIM
    return pl.pallas_call(
        _sb_kernel,
        out_shape=jax.ShapeDtypeStruct((t, d), BF16),
        grid=(nh, t // SB_BLOCK),
        in_specs=[pl.BlockSpec((SB_BLOCK, SB_HEAD_DIM), lambda h, i: (i, h)),
                  pl.BlockSpec((t, SB_HEAD_DIM), lambda h, i: (0, h)),
                  pl.BlockSpec((t, SB_HEAD_DIM), lambda h, i: (0, h))],
        out_specs=pl.BlockSpec((SB_BLOCK, SB_HEAD_DIM), lambda h, i: (i, h)),
        compiler_params=_cparams("parallel", "arbitrary"),
        name="sb_attention",
    )(q, k, v)


def _glu_up_kernel(x_ref, xh_ref, wg_ref, wu_ref, cw_ref, cb_ref, o_ref):
    i = pl.program_id(0)
    x = x_ref[...]
    wg = wg_ref[...].astype(BF16)
    gate = _dot(x, wg)
    up = _dot(x, wu_ref[...].astype(BF16))
    halo = _dot(xh_ref[...], wg) * (i > 0).astype(F32)
    row = lax.broadcasted_iota(jnp.int32, gate.shape, 0)
    h1 = halo[GLU_HALO - 1:GLU_HALO, :]
    h2 = halo[GLU_HALO - 2:GLU_HALO - 1, :]
    g1 = jnp.where(row == 0, h1, pltpu.roll(gate, 1, 0))
    g2 = jnp.where(row == 0, h2, jnp.where(row == 1, h1, pltpu.roll(gate, 2, 0)))
    cw = cw_ref[...]
    conv = cb_ref[...] + g2 * cw[0:1, :] + g1 * cw[1:2, :] + gate * cw[2:3, :]
    o_ref[...] = (jax.nn.silu(conv) * up).astype(o_ref.dtype)


def _glu_up(x, w_up, conv_w, conv_b, layer):
    t, d = x.shape
    f = w_up.shape[2] // 2
    bm = _pick(t, (1024, 512, 256, 128, 64, 32, 16))
    bn = _pick(f, (256, 128))
    nf = f // bn
    per = bm // GLU_HALO
    return pl.pallas_call(
        _glu_up_kernel,
        out_shape=jax.ShapeDtypeStruct((t, f), BF16),
        grid=(t // bm, nf),
        in_specs=[pl.BlockSpec((bm, d), lambda i, j: (i, 0)),
                  pl.BlockSpec((GLU_HALO, d), lambda i, j: (jnp.maximum(i * per - 1, 0), 0)),
                  pl.BlockSpec((None, d, bn), lambda i, j: (layer, 0, j)),
                  pl.BlockSpec((None, d, bn), lambda i, j: (layer, 0, j + nf)),
                  pl.BlockSpec((None, CONV_WIDTH, bn), lambda i, j: (layer, 0, j)),
                  pl.BlockSpec((None, 1, bn), lambda i, j: (layer, 0, j))],
        out_specs=pl.BlockSpec((bm, bn), lambda i, j: (i, j)),
        compiler_params=_cparams("parallel", "arbitrary"),
        name="glu_up",
    )(x, x, w_up, w_up, conv_w, conv_b.reshape(conv_b.shape[0], 1, f))


def kernel(x, ln_mix_w, ln_ffn_w, rwkv_mix, rwkv_w_r, rwkv_w_k, rwkv_w_v, rwkv_w_o, rwkv_decay_w0, rwkv_decay_w1, rwkv_decay_w2, rwkv_iclr_a0, rwkv_iclr_a1, rwkv_iclr_a2, rwkv_gate_g1, rwkv_gate_g2, rwkv_k_k, rwkv_k_a, rwkv_r_k, rwkv_lnx_w, rwkv_lnx_b, vres_v0, vres_v1, vres_v2, kv_norm_w, sb_w_k, sb_w_v, sb_w_q, sb_w_o, ffn_w_up, ffn_conv_w, ffn_conv_b, ffn_w_down, final_norm_w):
    bsz, seqlen, d = x.shape
    assert bsz == 1
    depth = ln_mix_w.shape[0]
    n_a = rwkv_mix.shape[0]
    f = ffn_w_down.shape[1]
    h = x.reshape(seqlen, d)
    zero_bias = jnp.zeros((d,), F32)
    sb_k3 = sb_w_k.reshape(1, d, d)
    sb_v3 = sb_w_v.reshape(1, d, d)
    bk_down = f // 2 if (f // 2) % LANES == 0 else f

    v_first = None
    shared_k = shared_v = None
    for layer in range(depth):
        if layer < n_a:
            i = layer
            xr, xw, xk, xv, xa, xg = _norm_mix(h, ln_mix_w[layer], rwkv_mix[i])
            r = _matmul(xr, rwkv_w_r, i, out_dtype=F32)
            k = _matmul(xk, rwkv_w_k, i, out_dtype=F32)
            v = _matmul(xv, rwkv_w_v, i, out_dtype=F32)
            wraw = _lora(xw, rwkv_decay_w1[i], rwkv_decay_w2[i], rwkv_decay_w0[i], mid="tanh")
            a = _lora(xa, rwkv_iclr_a1[i], rwkv_iclr_a2[i], rwkv_iclr_a0[i], post="sigmoid")
            g = _lora(xg, rwkv_gate_g1[i], rwkv_gate_g2[i], zero_bias, mid="sigmoid")
            if i == 0:
                v_first = v
                vgate = None
            else:
                vgate = _lora(xv, vres_v1[i - 1], vres_v2[i - 1], vres_v0[i - 1], post="sigmoid")
            mixed = _wkv(r, k, v, wraw, a, g, v_first if i > 0 else None, vgate,
                         rwkv_k_k[i], rwkv_k_a[i], rwkv_r_k[i], rwkv_lnx_w[i], rwkv_lnx_b[i])
            h = _matmul(mixed, rwkv_w_o, i, out_dtype=F32, res=h)
        else:
            j = layer - n_a
            if shared_k is None:
                kvn = _rmsnorm(h, kv_norm_w, BF16)
                shared_k = _matmul(kvn, sb_k3, 0, out_dtype=BF16)
                shared_v = _matmul(kvn, sb_v3, 0, out_dtype=BF16)
            hn = _rmsnorm(h, ln_mix_w[layer], BF16)
            q = _matmul(hn, sb_w_q, j, out_dtype=BF16)
            o = _sb_attention(q, shared_k, shared_v)
            h = _matmul(o, sb_w_o, j, out_dtype=F32, res=h)
        hn2 = _rmsnorm(h, ln_ffn_w[layer], BF16)
        mid = _glu_up(hn2, ffn_w_up, ffn_conv_w, ffn_conv_b, layer)
        h = _matmul(mid, ffn_w_down, layer, out_dtype=F32, res=h, bk=bk_down, bn=_pick(d, (256, 128)))
    out = _rmsnorm(h, final_norm_w, F32)
    return out.reshape(bsz, seqlen, d)
```

```python
import functools

import jax
import jax.numpy as jnp
from jax import lax
from jax.experimental import pallas as pl
from jax.experimental.pallas import tpu as pltpu

F32 = jnp.float32
BF16 = jnp.bfloat16

RWKV_HEAD = 64
LNX_EPS = 64e-5
SB_HEAD_DIM = 128
SB_BLOCK = 128
CONV_WIDTH = 3
RMS_EPS = 1e-6

LANES = 128
SUBLANES = 8
GLU_HALO = 2 * SUBLANES
VMEM_LIMIT_BYTES = 56 * 1024 * 1024

WKV_CHUNK = LANES // 2
SB_DEAD_LOG = -110.0


def _cparams(*sem):
    return pltpu.CompilerParams(dimension_semantics=sem, vmem_limit_bytes=VMEM_LIMIT_BYTES)


def _pick(n, prefs):
    for p in prefs:
        if n % p == 0:
            return p
    return n


def _dot(a, b, dims=(((1,), (0,)), ((), ())), precision=None):
    return lax.dot_general(a, b, dims, precision=precision, preferred_element_type=F32)


_NT = (((1,), (1,)), ((), ()))
_TN = (((0,), (0,)), ((), ()))
_HI = lax.Precision.HIGHEST


def _rms(x, w):
    return x * lax.rsqrt(jnp.mean(x * x, axis=-1, keepdims=True) + RMS_EPS) * w


def _rmsnorm_kernel(x_ref, w_ref, o_ref):
    o_ref[...] = _rms(x_ref[...], w_ref[...]).astype(o_ref.dtype)


def _rmsnorm(x, w, out_dtype):
    t, d = x.shape
    bt = _pick(t, (256, 128, 64, 32, 16))
    return pl.pallas_call(
        _rmsnorm_kernel,
        out_shape=jax.ShapeDtypeStruct((t, d), out_dtype),
        grid=(t // bt,),
        in_specs=[pl.BlockSpec((bt, d), lambda i: (i, 0)),
                  pl.BlockSpec((1, d), lambda i: (0, 0))],
        out_specs=pl.BlockSpec((bt, d), lambda i: (i, 0)),
        compiler_params=_cparams("parallel"),
        name="rmsnorm",
    )(x, w.reshape(1, d))


def _norm_mix_kernel(x_ref, xp_ref, w_ref, mix_ref, *o_refs):
    i = pl.program_id(0)
    w = w_ref[...]
    hn = _rms(x_ref[...], w)
    prev = _rms(xp_ref[SUBLANES - 1:SUBLANES, :], w) * (i > 0).astype(F32)
    row = lax.broadcasted_iota(jnp.int32, hn.shape, 0)
    shifted = jnp.where(row == 0, prev, pltpu.roll(hn, 1, 0))
    xx = shifted - hn
    for j, o_ref in enumerate(o_refs):
        o_ref[...] = (hn + xx * mix_ref[j:j + 1, :]).astype(o_ref.dtype)


def _norm_mix(x, w, mix):
    t, d = x.shape
    bt = _pick(t, (256, 128, 64, 32, 16))
    per = bt // SUBLANES
    n = mix.shape[0]
    return pl.pallas_call(
        _norm_mix_kernel,
        out_shape=[jax.ShapeDtypeStruct((t, d), BF16)] * n,
        grid=(t // bt,),
        in_specs=[pl.BlockSpec((bt, d), lambda i: (i, 0)),
                  pl.BlockSpec((SUBLANES, d), lambda i: (jnp.maximum(i * per - 1, 0), 0)),
                  pl.BlockSpec((1, d), lambda i: (0, 0)),
                  pl.BlockSpec((n, d), lambda i: (0, 0))],
        out_specs=[pl.BlockSpec((bt, d), lambda i: (i, 0))] * n,
        compiler_params=_cparams("parallel"),
        name="norm_mix",
    )(x, x, w.reshape(1, d), mix)


def _mm_kernel(*refs, nk, has_res):
    a_ref, w_ref = refs[0], refs[1]
    res_ref = refs[2] if has_res else None
    o_ref = refs[2 + has_res]
    acc_ref = refs[3 + has_res] if nk > 1 else None
    part = _dot(a_ref[...].astype(BF16), w_ref[...].astype(BF16))

    def finish(acc):
        if has_res:
            acc = acc + res_ref[...]
        o_ref[...] = acc.astype(o_ref.dtype)

    if nk == 1:
        finish(part)
    else:
        kk = pl.program_id(1)
        j = pl.program_id(2)

        @pl.when(kk == 0)
        def _():
            acc_ref[j] = part

        @pl.when(jnp.logical_and(kk > 0, kk < nk - 1))
        def _():
            acc_ref[j] += part

        @pl.when(kk == nk - 1)
        def _():
            finish(acc_ref[j] + part)


def _matmul(a, w, layer, *, out_dtype, res=None, bm=None, bn=None, bk=None):
    m, k = a.shape
    n = w.shape[2]
    bm = bm or _pick(m, (1024, 512, 256, 128, 64, 32, 16))
    bn = bn or _pick(n, (512, 256, 128))
    bk = bk or k
    nk = k // bk
    nj = n // bn
    assert m % bm == 0 and n % bn == 0 and k % bk == 0

    def out_idx(i, q, j):
        return (i, j) if nk == 1 else (i, jnp.where(q == nk - 1, j, 0))

    a_mode = {} if nk == 1 else {"pipeline_mode": pl.Buffered(1)}
    in_specs = [pl.BlockSpec((bm, bk), lambda i, q, j: (i, q), **a_mode),
                pl.BlockSpec((None, bk, bn), lambda i, q, j: (layer, q, j))]
    args = [a, w]
    aliases = {}
    if res is not None:
        in_specs.append(pl.BlockSpec((bm, bn), out_idx))
        args.append(res)
        aliases = {2: 0}
    return pl.pallas_call(
        functools.partial(_mm_kernel, nk=nk, has_res=res is not None),
        out_shape=jax.ShapeDtypeStruct((m, n), out_dtype),
        grid=(m // bm, nk, nj),
        in_specs=in_specs,
        out_specs=pl.BlockSpec((bm, bn), out_idx),
        scratch_shapes=[pltpu.VMEM((nj, bm, bn), F32)] if nk > 1 else [],
        input_output_aliases=aliases,
        compiler_params=_cparams("parallel", "arbitrary", "arbitrary"),
        name="matmul",
    )(*args)


def _lora_kernel(x_ref, w1_ref, w2_ref, b_ref, o_ref, *, mid, post):
    hmid = _dot(x_ref[...], w1_ref[...].astype(BF16))
    if mid == "tanh":
        hmid = jnp.tanh(hmid)
    elif mid == "sigmoid":
        hmid = jax.nn.sigmoid(hmid)
    out = _dot(hmid.astype(BF16), w2_ref[...].astype(BF16)) + b_ref[...]
    if post == "sigmoid":
        out = jax.nn.sigmoid(out)
    o_ref[...] = out.astype(o_ref.dtype)


def _lora(x, w1, w2, bias, *, mid=None, post=None):
    t, d = x.shape
    r = w1.shape[1]
    rp = -(-r // LANES) * LANES
    if rp != r:
        w1 = jnp.pad(w1, ((0, 0), (0, rp - r)))
        w2 = jnp.pad(w2, ((0, rp - r), (0, 0)))
    bt = _pick(t, (512, 256, 128, 64, 32, 16))
    return pl.pallas_call(
        functools.partial(_lora_kernel, mid=mid, post=post),
        out_shape=jax.ShapeDtypeStruct((t, d), F32),
        grid=(t // bt,),
        in_specs=[pl.BlockSpec((bt, d), lambda i: (i, 0)),
                  pl.BlockSpec((d, rp), lambda i: (0, 0)),
                  pl.BlockSpec((rp, d), lambda i: (0, 0)),
                  pl.BlockSpec((1, d), lambda i: (0, 0))],
        out_specs=pl.BlockSpec((bt, d), lambda i: (i, 0)),
        compiler_params=_cparams("parallel"),
        name="lora",
    )(x, w1, w2, bias.reshape(1, d))


def _split3(x):
    hi = x.astype(BF16)
    r1 = x - hi.astype(F32)
    mid = r1.astype(BF16)
    lo = (r1 - mid.astype(F32)).astype(BF16)
    return hi, mid, lo


def _dot_exact_rhs(x, m_bf16):
    hi, mid, lo = _split3(x)
    return _dot(hi, m_bf16) + _dot(mid, m_bf16) + _dot(lo, m_bf16)


def _softplus(x):
    return jnp.maximum(x, 0.0) + jnp.log1p(jnp.exp(-jnp.abs(x)))


def _wkv_kernel(*refs, pairs, mix_v):
    if mix_v:
        (r_ref, k_ref, v_ref, wr_ref, a_ref, g_ref, vf_ref, vg_ref,
         kk_ref, ka_ref, rk_ref, lnw_ref, lnb_ref, o_ref, state_ref) = refs
    else:
        (r_ref, k_ref, v_ref, wr_ref, a_ref, g_ref,
         kk_ref, ka_ref, rk_ref, lnw_ref, lnb_ref, o_ref, state_ref) = refs
    c = WKV_CHUNK
    n2 = 2 * c

    @pl.when(pl.program_id(1) == 0)
    def _():
        state_ref[...] = jnp.zeros_like(state_ref)

    row = lax.broadcasted_iota(jnp.int32, (n2, n2), 0)
    col = lax.broadcasted_iota(jnp.int32, (n2, n2), 1)
    same = (row < c) == (col < c)
    strict = jnp.logical_and(same, col < row)
    incl = jnp.logical_and(same, col <= row)
    m0 = (lax.broadcasted_iota(jnp.int32, (c, n2), 1) < RWKV_HEAD).astype(F32)
    m1 = 1.0 - m0
    blk_ones = same.astype(BF16)
    tri = (lax.broadcasted_iota(jnp.int32, (c, c), 0)
           >= lax.broadcasted_iota(jnp.int32, (c, c), 1)).astype(BF16)
    eye = (row == col).astype(F32)

    def stack(x):
        return jnp.concatenate([x * m0, x * m1], axis=0)

    def unstack(xs):
        return xs[:c] + xs[c:]

    def to_rows(x):
        return jnp.concatenate([x[:, p * LANES:(p + 1) * LANES] for p in range(pairs)], axis=0)

    def to_lanes(x):
        return jnp.concatenate([x[p * c:(p + 1) * c] for p in range(pairs)], axis=1)

    def head_sum(x):
        return to_lanes(_dot_exact_rhs(to_rows(x), blk_ones))

    r = r_ref[...]
    k = k_ref[...]
    v = v_ref[...]
    a = a_ref[...]
    if mix_v:
        v = v + (vf_ref[...] - v) * vg_ref[...]
    lw = -jnp.exp(-_softplus(-wr_ref[...]) - 0.5)
    kraw = k * kk_ref[...]
    kn = kraw / jnp.maximum(jnp.sqrt(head_sum(kraw * kraw)), 1e-12)
    k2 = k * (1.0 + (a - 1.0) * ka_ref[...])
    lw_hi, lw_mid, lw_lo = _split3(lw)
    cum = _dot(tri, lw_hi) + _dot(tri, lw_mid) + _dot(tri, lw_lo)
    g_in = jnp.exp(cum)
    g_inv = jnp.exp(-cum)
    ah_all = -kn * jnp.exp(cum - lw)
    rh_all = r * g_in
    bh_all = kn * a * g_inv
    kh_all = k2 * g_inv
    bonus = head_sum(r * k2 * rk_ref[...]) * v

    def per_pair(fn, *xs):
        return [fn(*[x[q] for x in xs]) for q in range(pairs)]

    def lanes(x):
        return [x[:, q * LANES:(q + 1) * LANES] for q in range(pairs)]

    g_end = [x[c - 1:c, :] for x in lanes(g_in)]
    vs = per_pair(lambda x: stack(x).astype(BF16), lanes(v))
    lhs = per_pair(lambda x, y: jnp.concatenate([stack(x), stack(y)], axis=0).astype(BF16),
                   lanes(ah_all), lanes(rh_all))
    rhs_f = per_pair(lambda x, y: jnp.concatenate([stack(x), stack(y)], axis=0),
                     lanes(bh_all), lanes(kh_all))
    gram = per_pair(lambda x, y: _dot(x, y.astype(BF16), _NT), lhs, rhs_f)
    a_ab = [jnp.where(strict, gm[:n2, :n2], 0.0) for gm in gram]
    a_low = [jnp.concatenate([jnp.where(strict, gm[:n2, n2:], 0.0),
                              jnp.where(incl, gm[n2:, n2:], 0.0)], axis=0).astype(BF16) for gm in gram]
    a_rb = [jnp.where(incl, gm[n2:, :n2], 0.0).astype(BF16) for gm in gram]

    rinv = [eye + x for x in a_ab]
    pw = per_pair(lambda x: _dot(x.astype(BF16), x.astype(BF16)), a_ab)
    levels = (c - 1).bit_length()
    for lvl in range(1, levels):
        if lvl < levels - 1:
            both = per_pair(lambda pq, rq: _dot(pq.astype(BF16),
                                                jnp.concatenate([rq, pq], axis=1).astype(BF16)), pw, rinv)
            rinv = [rq + bq[:, :n2] for rq, bq in zip(rinv, both)]
            pw = [bq[:, n2:] for bq in both]
        else:
            rinv = per_pair(lambda pq, rq: rq + _dot(pq.astype(BF16), rq.astype(BF16)), pw, rinv)

    s_t = [state_ref[q] for q in range(pairs)]
    from_state = per_pair(lambda x, s: _dot(x, s.astype(BF16), _NT), lhs, s_t)
    from_v = per_pair(_dot, a_low, vs)
    us = per_pair(lambda rq, fs, fv: _dot(rq.astype(BF16), (fs[:n2] + fv[:n2]).astype(BF16)).astype(BF16),
                  rinv, from_state, from_v)
    ys = per_pair(lambda fs, fv, ab, u: fs[n2:] + fv[n2:] + _dot(ab, u), from_state, from_v, a_rb, us)
    upd = per_pair(lambda u, vq, rf, ge: _dot(jnp.concatenate([u, vq], axis=0), (rf * ge).astype(BF16), _TN),
                   us, vs, rhs_f, g_end)
    for q in range(pairs):
        state_ref[q] = jnp.where(same, s_t[q] * g_end[q] + upd[q], 0.0)
    y = jnp.concatenate([unstack(x) for x in ys], axis=1)

    inv_n = 1.0 / RWKV_HEAD
    dlt = y - head_sum(y) * inv_n
    var = head_sum(dlt * dlt) * inv_n
    yn = dlt * lax.rsqrt(var + LNX_EPS) * lnw_ref[...] + lnb_ref[...]
    o_ref[...] = ((yn + bonus) * g_ref[...]).astype(o_ref.dtype)


def _wkv(r, k, v, wraw, a, g, vfirst, vgate, k_k, k_a, r_k, lnx_w, lnx_b, *, pairs=None):
    t, d = r.shape
    c = WKV_CHUNK
    npairs = d // LANES
    pairs = pairs or _pick(npairs, (16, 8, 4, 2, 1))
    wd = pairs * LANES
    mix_v = vfirst is not None
    seqs = [r, k, v, wraw, a, g] + ([vfirst, vgate] if mix_v else [])
    seq = pl.BlockSpec((c, wd), lambda h, i: (i, h))
    par = pl.BlockSpec((1, wd), lambda h, i: (0, h))
    prm = [x.reshape(1, d) for x in (k_k, k_a, r_k, lnx_w, lnx_b)]
    return pl.pallas_call(
        functools.partial(_wkv_kernel, pairs=pairs, mix_v=mix_v),
        out_shape=jax.ShapeDtypeStruct((t, d), BF16),
        grid=(npairs // pairs, t // c),
        in_specs=[seq] * len(seqs) + [par] * len(prm),
        out_specs=seq,
        scratch_shapes=[pltpu.VMEM((pairs, LANES, LANES), F32)],
        compiler_params=_cparams("parallel", "arbitrary"),
        name="wkv7",
    )(*seqs, *prm)


def _sb_kernel(q_ref, k_ref, v_ref, o_ref, *, heads):
    blk = SB_BLOCK
    hd = SB_HEAD_DIM
    qb = pl.program_id(1)
    scale = 1.0 / (SB_HEAD_DIM ** 0.5)
    row = lax.broadcasted_iota(jnp.int32, (blk, blk), 0)
    col = lax.broadcasted_iota(jnp.int32, (blk, blk), 1)
    tri = jnp.concatenate([(row > col).astype(BF16), jnp.ones((blk, blk), BF16)], axis=1)
    lanes = [slice(h * hd, (h + 1) * hd) for h in range(heads)]
    qs = [q_ref[:, sl] for sl in lanes]

    def cond(st):
        kb, live = st[0], st[1]
        return jnp.logical_and(kb >= 0, live > 0)

    def body(st):
        kb, _, accs, carries = st
        off = pl.multiple_of(kb * blk, blk)
        strict = (col + kb * blk) < (row + qb * blk)
        zs = [_dot(q, k_ref[pl.ds(off, blk), sl], _NT) * scale for q, sl in zip(qs, lanes)]
        sps = [_softplus(z) for z in zs]
        l1m = jnp.concatenate([jnp.where(strict, -sp, 0.0) for sp in sps], axis=0)
        hi = l1m.astype(BF16)
        lo = (l1m - hi.astype(F32)).astype(BF16)
        cs = _dot(hi, tri) + _dot(lo, tri)
        css = [cs[h * blk:(h + 1) * blk] for h in range(heads)]
        attn = [jnp.where(strict, jnp.exp((z - sp) + c[:, :blk] + cr), 0.0).astype(BF16)
                for z, sp, c, cr in zip(zs, sps, css, carries)]
        accs = tuple(acc + _dot(p, v_ref[pl.ds(off, blk), sl]) for acc, p, sl in zip(accs, attn, lanes))
        carries = tuple(cr + c[:, blk:] for cr, c in zip(carries, css))
        top = functools.reduce(jnp.maximum, carries)
        live = (jnp.max(top) > SB_DEAD_LOG).astype(jnp.int32)
        return kb - 1, live, accs, carries

    zeros = tuple(jnp.zeros((blk, blk), F32) for _ in range(heads))
    st = lax.while_loop(cond, body, (qb, jnp.int32(1), zeros, zeros))
    for acc, sl in zip(st[2], lanes):
        o_ref[:, sl] = acc.astype(o_ref.dtype)


def _sb_attention(q, k, v):
    t, d = q.shape
    nh = d // SB_HEAD_DIM
    heads = _pick(nh, (4, 2, 1))
    wd = heads * SB_HEAD_DIM
    return pl.pallas_call(
        functools.partial(_sb_kernel, heads=heads),
        out_shape=jax.ShapeDtypeStruct((t, d), BF16),
        grid=(nh // heads, t // SB_BLOCK),
        in_specs=[pl.BlockSpec((SB_BLOCK, wd), lambda h, i: (i, h)),
                  pl.BlockSpec((t, wd), lambda h, i: (0, h)),
                  pl.BlockSpec((t, wd), lambda h, i: (0, h))],
        out_specs=pl.BlockSpec((SB_BLOCK, wd), lambda h, i: (i, h)),
        compiler_params=_cparams("parallel", "arbitrary"),
        name="sb_attention",
    )(q, k, v)


def _glu_up_kernel(x_ref, xh_ref, wg_ref, wu_ref, cw_ref, cb_ref, o_ref):
    i = pl.program_id(0)
    x = x_ref[...]
    wg = wg_ref[...].astype(BF16)
    gate = _dot(x, wg)
    up = _dot(x, wu_ref[...].astype(BF16))
    halo = _dot(xh_ref[...], wg) * (i > 0).astype(F32)
    row = lax.broadcasted_iota(jnp.int32, gate.shape, 0)
    h1 = halo[GLU_HALO - 1:GLU_HALO, :]
    h2 = halo[GLU_HALO - 2:GLU_HALO - 1, :]
    g1 = jnp.where(row == 0, h1, pltpu.roll(gate, 1, 0))
    g2 = jnp.where(row == 0, h2, jnp.where(row == 1, h1, pltpu.roll(gate, 2, 0)))
    cw = cw_ref[...]
    conv = cb_ref[...] + g2 * cw[0:1, :] + g1 * cw[1:2, :] + gate * cw[2:3, :]
    o_ref[...] = (jax.nn.silu(conv) * up).astype(o_ref.dtype)


def _glu_up(x, w_up, conv_w, conv_b, layer):
    t, d = x.shape
    f = w_up.shape[2] // 2
    bm = _pick(t, (1024, 512, 256, 128, 64, 32, 16))
    bn = _pick(f, (256, 128))
    nf = f // bn
    per = bm // GLU_HALO
    return pl.pallas_call(
        _glu_up_kernel,
        out_shape=jax.ShapeDtypeStruct((t, f), BF16),
        grid=(t // bm, nf),
        in_specs=[pl.BlockSpec((bm, d), lambda i, j: (i, 0)),
                  pl.BlockSpec((GLU_HALO, d), lambda i, j: (jnp.maximum(i * per - 1, 0), 0)),
                  pl.BlockSpec((None, d, bn), lambda i, j: (layer, 0, j)),
                  pl.BlockSpec((None, d, bn), lambda i, j: (layer, 0, j + nf)),
                  pl.BlockSpec((None, CONV_WIDTH, bn), lambda i, j: (layer, 0, j)),
                  pl.BlockSpec((None, 1, bn), lambda i, j: (layer, 0, j))],
        out_specs=pl.BlockSpec((bm, bn), lambda i, j: (i, j)),
        compiler_params=_cparams("parallel", "arbitrary"),
        name="glu_up",
    )(x, x, w_up, w_up, conv_w, conv_b.reshape(conv_b.shape[0], 1, f))


def kernel(x, ln_mix_w, ln_ffn_w, rwkv_mix, rwkv_w_r, rwkv_w_k, rwkv_w_v, rwkv_w_o, rwkv_decay_w0, rwkv_decay_w1, rwkv_decay_w2, rwkv_iclr_a0, rwkv_iclr_a1, rwkv_iclr_a2, rwkv_gate_g1, rwkv_gate_g2, rwkv_k_k, rwkv_k_a, rwkv_r_k, rwkv_lnx_w, rwkv_lnx_b, vres_v0, vres_v1, vres_v2, kv_norm_w, sb_w_k, sb_w_v, sb_w_q, sb_w_o, ffn_w_up, ffn_conv_w, ffn_conv_b, ffn_w_down, final_norm_w):
    bsz, seqlen, d = x.shape
    assert bsz == 1
    depth = ln_mix_w.shape[0]
    n_a = rwkv_mix.shape[0]
    f = ffn_w_down.shape[1]
    h = x.reshape(seqlen, d)
    zero_bias = jnp.zeros((d,), F32)
    sb_k3 = sb_w_k.reshape(1, d, d)
    sb_v3 = sb_w_v.reshape(1, d, d)
    bk_down = f // 2 if (f // 2) % LANES == 0 else f

    v_first = None
    shared_k = shared_v = None
    for layer in range(depth):
        if layer < n_a:
            i = layer
            xr, xw, xk, xv, xa, xg = _norm_mix(h, ln_mix_w[layer], rwkv_mix[i])
            r = _matmul(xr, rwkv_w_r, i, out_dtype=F32)
            k = _matmul(xk, rwkv_w_k, i, out_dtype=F32)
            v = _matmul(xv, rwkv_w_v, i, out_dtype=F32)
            wraw = _lora(xw, rwkv_decay_w1[i], rwkv_decay_w2[i], rwkv_decay_w0[i], mid="tanh")
            a = _lora(xa, rwkv_iclr_a1[i], rwkv_iclr_a2[i], rwkv_iclr_a0[i], post="sigmoid")
            g = _lora(xg, rwkv_gate_g1[i], rwkv_gate_g2[i], zero_bias, mid="sigmoid")
            if i == 0:
                v_first = v
                vgate = None
            else:
                vgate = _lora(xv, vres_v1[i - 1], vres_v2[i - 1], vres_v0[i - 1], post="sigmoid")
            mixed = _wkv(r, k, v, wraw, a, g, v_first if i > 0 else None, vgate,
                         rwkv_k_k[i], rwkv_k_a[i], rwkv_r_k[i], rwkv_lnx_w[i], rwkv_lnx_b[i])
            h = _matmul(mixed, rwkv_w_o, i, out_dtype=F32, res=h)
        else:
            j = layer - n_a
            if shared_k is None:
                kvn = _rmsnorm(h, kv_norm_w, BF16)
                shared_k = _matmul(kvn, sb_k3, 0, out_dtype=BF16)
                shared_v = _matmul(kvn, sb_v3, 0, out_dtype=BF16)
            hn = _rmsnorm(h, ln_mix_w[layer], BF16)
            q = _matmul(hn, sb_w_q, j, out_dtype=BF16)
            o = _sb_attention(q, shared_k, shared_v)
            h = _matmul(o, sb_w_o, j, out_dtype=F32, res=h)
        hn2 = _rmsnorm(h, ln_ffn_w[layer], BF16)
        mid = _glu_up(hn2, ffn_w_up, ffn_conv_w, ffn_conv_b, layer)
        h = _matmul(mid, ffn_w_down, layer, out_dtype=F32, res=h, bk=bk_down, bn=_pick(d, (256, 128)))
    out = _rmsnorm(h, final_norm_w, F32)
    return out.reshape(bsz, seqlen, d)
```

```python
import functools

import jax
import jax.numpy as jnp
from jax import lax
from jax.experimental import pallas as pl
from jax.experimental.pallas import tpu as pltpu

F32 = jnp.float32
BF16 = jnp.bfloat16

RWKV_HEAD = 64
LNX_EPS = 64e-5
SB_HEAD_DIM = 128
SB_BLOCK = 128
CONV_WIDTH = 3
RMS_EPS = 1e-6

LANES = 128
SUBLANES = 8
GLU_HALO = 2 * SUBLANES
VMEM_LIMIT_BYTES = 56 * 1024 * 1024

WKV_CHUNK = LANES // 2
SB_DEAD_LOG = -110.0


def _cparams(*sem):
    return pltpu.CompilerParams(dimension_semantics=sem, vmem_limit_bytes=VMEM_LIMIT_BYTES)


def _pick(n, prefs):
    for p in prefs:
        if n % p == 0:
            return p
    return n


def _dot(a, b, dims=(((1,), (0,)), ((), ())), precision=None):
    return lax.dot_general(a, b, dims, precision=precision, preferred_element_type=F32)


_NT = (((1,), (1,)), ((), ()))
_TN = (((0,), (0,)), ((), ()))
_HI = lax.Precision.HIGHEST


def _rms(x, w):
    return x * lax.rsqrt(jnp.mean(x * x, axis=-1, keepdims=True) + RMS_EPS) * w


def _rmsnorm_kernel(x_ref, w_ref, o_ref):
    o_ref[...] = _rms(x_ref[...], w_ref[...]).astype(o_ref.dtype)


def _rmsnorm(x, w, out_dtype):
    t, d = x.shape
    bt = _pick(t, (256, 128, 64, 32, 16))
    return pl.pallas_call(
        _rmsnorm_kernel,
        out_shape=jax.ShapeDtypeStruct((t, d), out_dtype),
        grid=(t // bt,),
        in_specs=[pl.BlockSpec((bt, d), lambda i: (i, 0)),
                  pl.BlockSpec((1, d), lambda i: (0, 0))],
        out_specs=pl.BlockSpec((bt, d), lambda i: (i, 0)),
        compiler_params=_cparams("parallel"),
        name="rmsnorm",
    )(x, w.reshape(1, d))


def _norm_mix_kernel(x_ref, xp_ref, w_ref, mix_ref, *o_refs):
    i = pl.program_id(0)
    w = w_ref[...]
    hn = _rms(x_ref[...], w)
    prev = _rms(xp_ref[SUBLANES - 1:SUBLANES, :], w) * (i > 0).astype(F32)
    row = lax.broadcasted_iota(jnp.int32, hn.shape, 0)
    shifted = jnp.where(row == 0, prev, pltpu.roll(hn, 1, 0))
    xx = shifted - hn
    for j, o_ref in enumerate(o_refs):
        o_ref[...] = (hn + xx * mix_ref[j:j + 1, :]).astype(o_ref.dtype)


def _norm_mix(x, w, mix):
    t, d = x.shape
    bt = _pick(t, (256, 128, 64, 32, 16))
    per = bt // SUBLANES
    n = mix.shape[0]
    return pl.pallas_call(
        _norm_mix_kernel,
        out_shape=[jax.ShapeDtypeStruct((t, d), BF16)] * n,
        grid=(t // bt,),
        in_specs=[pl.BlockSpec((bt, d), lambda i: (i, 0)),
                  pl.BlockSpec((SUBLANES, d), lambda i: (jnp.maximum(i * per - 1, 0), 0)),
                  pl.BlockSpec((1, d), lambda i: (0, 0)),
                  pl.BlockSpec((n, d), lambda i: (0, 0))],
        out_specs=[pl.BlockSpec((bt, d), lambda i: (i, 0))] * n,
        compiler_params=_cparams("parallel"),
        name="norm_mix",
    )(x, x, w.reshape(1, d), mix)


def _mm_kernel(*refs, nk, has_res):
    a_ref, w_ref = refs[0], refs[1]
    res_ref = refs[2] if has_res else None
    o_ref = refs[2 + has_res]
    acc_ref = refs[3 + has_res] if nk > 1 else None
    part = _dot(a_ref[...], w_ref[...].astype(BF16))

    def finish(acc):
        if has_res:
            acc = acc + res_ref[...]
        o_ref[...] = acc.astype(o_ref.dtype)

    if nk == 1:
        finish(part)
    else:
        kk = pl.program_id(1)
        j = pl.program_id(2)

        @pl.when(kk == 0)
        def _():
            acc_ref[j] = part

        @pl.when(jnp.logical_and(kk > 0, kk < nk - 1))
        def _():
            acc_ref[j] += part

        @pl.when(kk == nk - 1)
        def _():
            finish(acc_ref[j] + part)


def _matmul(a, w, layer, *, out_dtype, res=None, in_place=True, bm=None, bn=None, bk=None):
    m, k = a.shape
    n = w.shape[2]
    bm = bm or _pick(m, (1024, 512, 256, 128, 64, 32, 16))
    bn = bn or _pick(n, (512, 256, 128))
    bk = bk or k
    nk = k // bk
    nj = n // bn
    assert m % bm == 0 and n % bn == 0 and k % bk == 0

    def out_idx(i, q, j):
        return (i, j) if nk == 1 else (i, jnp.where(q == nk - 1, j, 0))

    a_mode = {} if nk == 1 else {"pipeline_mode": pl.Buffered(1)}
    in_specs = [pl.BlockSpec((bm, bk), lambda i, q, j: (i, q), **a_mode),
                pl.BlockSpec((None, bk, bn), lambda i, q, j: (layer, q, j))]
    args = [a, w]
    aliases = {}
    if res is not None:
        in_specs.append(pl.BlockSpec((bm, bn), out_idx))
        args.append(res)
        aliases = {2: 0} if in_place else {}
    return pl.pallas_call(
        functools.partial(_mm_kernel, nk=nk, has_res=res is not None),
        out_shape=jax.ShapeDtypeStruct((m, n), out_dtype),
        grid=(m // bm, nk, nj),
        in_specs=in_specs,
        out_specs=pl.BlockSpec((bm, bn), out_idx),
        scratch_shapes=[pltpu.VMEM((nj, bm, bn), F32)] if nk > 1 else [],
        input_output_aliases=aliases,
        compiler_params=_cparams("parallel", "arbitrary", "arbitrary"),
        name="matmul",
    )(*args)


def _lora_kernel(x_ref, w1_ref, w2_ref, b_ref, o_ref, *, mid, post):
    hmid = _dot(x_ref[...], w1_ref[...].astype(BF16))
    if mid == "tanh":
        hmid = jnp.tanh(hmid)
    elif mid == "sigmoid":
        hmid = jax.nn.sigmoid(hmid)
    out = _dot(hmid.astype(BF16), w2_ref[...].astype(BF16)) + b_ref[...]
    if post == "sigmoid":
        out = jax.nn.sigmoid(out)
    o_ref[...] = out.astype(o_ref.dtype)


def _lora(x, w1, w2, bias, *, mid=None, post=None):
    t, d = x.shape
    r = w1.shape[1]
    rp = -(-r // LANES) * LANES
    if rp != r:
        w1 = jnp.pad(w1, ((0, 0), (0, rp - r)))
        w2 = jnp.pad(w2, ((0, rp - r), (0, 0)))
    bt = _pick(t, (512, 256, 128, 64, 32, 16))
    return pl.pallas_call(
        functools.partial(_lora_kernel, mid=mid, post=post),
        out_shape=jax.ShapeDtypeStruct((t, d), F32),
        grid=(t // bt,),
        in_specs=[pl.BlockSpec((bt, d), lambda i: (i, 0)),
                  pl.BlockSpec((d, rp), lambda i: (0, 0)),
                  pl.BlockSpec((rp, d), lambda i: (0, 0)),
                  pl.BlockSpec((1, d), lambda i: (0, 0))],
        out_specs=pl.BlockSpec((bt, d), lambda i: (i, 0)),
        compiler_params=_cparams("parallel"),
        name="lora",
    )(x, w1, w2, bias.reshape(1, d))


def _split3(x):
    hi = x.astype(BF16)
    r1 = x - hi.astype(F32)
    mid = r1.astype(BF16)
    lo = (r1 - mid.astype(F32)).astype(BF16)
    return hi, mid, lo


def _dot_01(x, m_bf16):
    hi = x.astype(BF16)
    lo = (x - hi.astype(F32)).astype(BF16)
    return _dot(hi, m_bf16) + _dot(lo, m_bf16)


def _softplus(x):
    return jnp.maximum(x, 0.0) + jnp.log(1.0 + jnp.exp(-jnp.abs(x)))


def _wkv_kernel(*refs, pairs, mix_v):
    if mix_v:
        (r_ref, k_ref, v_ref, wr_ref, a_ref, g_ref, vf_ref, vg_ref,
         kk_ref, ka_ref, rk_ref, lnw_ref, lnb_ref, o_ref, state_ref) = refs
    else:
        (r_ref, k_ref, v_ref, wr_ref, a_ref, g_ref,
         kk_ref, ka_ref, rk_ref, lnw_ref, lnb_ref, o_ref, state_ref) = refs
    c = WKV_CHUNK
    n2 = 2 * c

    @pl.when(pl.program_id(1) == 0)
    def _():
        state_ref[...] = jnp.zeros_like(state_ref)

    row = lax.broadcasted_iota(jnp.int32, (n2, n2), 0)
    col = lax.broadcasted_iota(jnp.int32, (n2, n2), 1)
    same = (row < c) == (col < c)
    strict = jnp.logical_and(same, col < row)
    incl = jnp.logical_and(same, col <= row)
    first_head = lax.broadcasted_iota(jnp.int32, (c, n2), 1) < RWKV_HEAD
    m0b = first_head.astype(BF16)
    m1b = jnp.logical_not(first_head).astype(BF16)
    blk_ones = same.astype(BF16)
    tri = (lax.broadcasted_iota(jnp.int32, (c, c), 0)
           >= lax.broadcasted_iota(jnp.int32, (c, c), 1)).astype(BF16)
    eye = (row == col).astype(F32)

    def unstack(xs):
        return xs[:c] + xs[c:]

    def to_rows(x):
        return jnp.concatenate([x[:, p * LANES:(p + 1) * LANES] for p in range(pairs)], axis=0)

    def to_lanes(x):
        return jnp.concatenate([x[p * c:(p + 1) * c] for p in range(pairs)], axis=1)

    def head_sum(x):
        return to_lanes(_dot_01(to_rows(x), blk_ones))

    r = r_ref[...]
    k = k_ref[...]
    v = v_ref[...]
    a = a_ref[...]
    if mix_v:
        v = v + (vf_ref[...] - v) * vg_ref[...]
    lw = -jnp.exp(-_softplus(-wr_ref[...]) - 0.5)
    kraw = k * kk_ref[...]
    kn = kraw / jnp.maximum(jnp.sqrt(head_sum(kraw * kraw)), 1e-12)
    k2 = k * (1.0 + (a - 1.0) * ka_ref[...])
    lw_hi, lw_mid, lw_lo = _split3(lw)
    cum = _dot(tri, lw_hi) + _dot(tri, lw_mid) + _dot(tri, lw_lo)
    g_in = jnp.exp(cum)
    g_inv = jnp.exp(-cum)
    ah_all = -kn * jnp.exp(cum - lw)
    rh_all = r * g_in
    bh_all = kn * a * g_inv
    kh_all = k2 * g_inv
    bonus = head_sum(r * k2 * rk_ref[...]) * v

    def per_pair(fn, *xs):
        return [fn(*[x[q] for x in xs]) for q in range(pairs)]

    def lanes(x):
        return [x[:, q * LANES:(q + 1) * LANES] for q in range(pairs)]

    def stack2(x, y):
        return jnp.concatenate([x * m0b, x * m1b, y * m0b, y * m1b], axis=0)

    g_last = g_in[c - 1:c, :]
    g_end = lanes(g_last)
    vs = per_pair(lambda x: jnp.concatenate([x * m0b, x * m1b], axis=0), lanes(v.astype(BF16)))
    lhs = per_pair(stack2, lanes(ah_all.astype(BF16)), lanes(rh_all.astype(BF16)))
    rhs = per_pair(stack2, lanes(bh_all.astype(BF16)), lanes(kh_all.astype(BF16)))
    rhs_end = per_pair(stack2, lanes((bh_all * g_last).astype(BF16)),
                       lanes((kh_all * g_last).astype(BF16)))
    gram = per_pair(lambda x, y: _dot(x, y, _NT), lhs, rhs)
    a_ab = [jnp.where(strict, gm[:n2, :n2], 0.0) for gm in gram]
    a_low = [jnp.concatenate([jnp.where(strict, gm[:n2, n2:], 0.0),
                              jnp.where(incl, gm[n2:, n2:], 0.0)], axis=0).astype(BF16) for gm in gram]
    a_rb = [jnp.where(incl, gm[n2:, :n2], 0.0).astype(BF16) for gm in gram]

    rinv = [eye + x for x in a_ab]
    pw = per_pair(lambda x: _dot(x.astype(BF16), x.astype(BF16)), a_ab)
    levels = (c - 1).bit_length()
    for lvl in range(1, levels):
        if lvl < levels - 1:
            both = per_pair(lambda pq, rq: _dot(pq.astype(BF16),
                                                jnp.concatenate([rq, pq], axis=1).astype(BF16)), pw, rinv)
            rinv = [rq + bq[:, :n2] for rq, bq in zip(rinv, both)]
            pw = [bq[:, n2:] for bq in both]
        else:
            rinv = per_pair(lambda pq, rq: rq + _dot(pq.astype(BF16), rq.astype(BF16)), pw, rinv)

    s_t = [state_ref[q] for q in range(pairs)]
    from_state = per_pair(lambda x, s: _dot(x, s.astype(BF16), _NT), lhs, s_t)
    from_v = per_pair(_dot, a_low, vs)
    us = per_pair(lambda rq, fs, fv: _dot(rq.astype(BF16), (fs[:n2] + fv[:n2]).astype(BF16)).astype(BF16),
                  rinv, from_state, from_v)
    ys = per_pair(lambda fs, fv, ab, u: fs[n2:] + fv[n2:] + _dot(ab, u), from_state, from_v, a_rb, us)
    upd = per_pair(lambda u, vq, re: _dot(jnp.concatenate([u, vq], axis=0), re, _TN), us, vs, rhs_end)
    for q in range(pairs):
        state_ref[q] = jnp.where(same, s_t[q] * g_end[q] + upd[q], 0.0)
    y = jnp.concatenate([unstack(x) for x in ys], axis=1)

    inv_n = 1.0 / RWKV_HEAD
    dlt = y - head_sum(y) * inv_n
    var = head_sum(dlt * dlt) * inv_n
    yn = dlt * lax.rsqrt(var + LNX_EPS) * lnw_ref[...] + lnb_ref[...]
    o_ref[...] = ((yn + bonus) * g_ref[...]).astype(o_ref.dtype)


def _wkv(r, k, v, wraw, a, g, vfirst, vgate, k_k, k_a, r_k, lnx_w, lnx_b, *, pairs=None):
    t, d = r.shape
    c = WKV_CHUNK
    npairs = d // LANES
    pairs = pairs or _pick(npairs, (16, 8, 4, 2, 1))
    wd = pairs * LANES
    mix_v = vfirst is not None
    seqs = [r, k, v, wraw, a, g] + ([vfirst, vgate] if mix_v else [])
    seq = pl.BlockSpec((c, wd), lambda h, i: (i, h))
    par = pl.BlockSpec((1, wd), lambda h, i: (0, h))
    prm = [x.reshape(1, d) for x in (k_k, k_a, r_k, lnx_w, lnx_b)]
    return pl.pallas_call(
        functools.partial(_wkv_kernel, pairs=pairs, mix_v=mix_v),
        out_shape=jax.ShapeDtypeStruct((t, d), BF16),
        grid=(npairs // pairs, t // c),
        in_specs=[seq] * len(seqs) + [par] * len(prm),
        out_specs=seq,
        scratch_shapes=[pltpu.VMEM((pairs, LANES, LANES), F32)],
        compiler_params=_cparams("parallel", "arbitrary"),
        name="wkv7",
    )(*seqs, *prm)


def _sb_kernel(q_ref, k_ref, v_ref, o_ref, *, heads):
    blk = SB_BLOCK
    hd = SB_HEAD_DIM
    qb = pl.program_id(1)
    scale = 1.0 / (SB_HEAD_DIM ** 0.5)
    row = lax.broadcasted_iota(jnp.int32, (blk, blk), 0)
    col = lax.broadcasted_iota(jnp.int32, (blk, blk), 1)
    tri = jnp.concatenate([(row > col).astype(BF16), jnp.ones((blk, blk), BF16)], axis=1)
    lanes = [slice(h * hd, (h + 1) * hd) for h in range(heads)]
    qs = [q_ref[:, sl] for sl in lanes]

    def cond(st):
        kb, live = st[0], st[1]
        return jnp.logical_and(kb >= 0, live > 0)

    def sweep(st, diagonal):
        kb, _, accs, carries = st
        off = pl.multiple_of(kb * blk, blk)
        keep = (lambda x: jnp.where(col < row, x, 0.0)) if diagonal else (lambda x: x)
        zs = [_dot(q, k_ref[pl.ds(off, blk), sl], _NT) * scale for q, sl in zip(qs, lanes)]
        sps = [_softplus(z) for z in zs]
        l1m = jnp.concatenate([keep(-sp) for sp in sps], axis=0)
        hi = l1m.astype(BF16)
        lo = (l1m - hi.astype(F32)).astype(BF16)
        cs = _dot(hi, tri) + _dot(lo, tri)
        css = [cs[h * blk:(h + 1) * blk] for h in range(heads)]
        attn = [keep(jnp.exp((z - sp) + c[:, :blk] + cr)).astype(BF16)
                for z, sp, c, cr in zip(zs, sps, css, carries)]
        accs = tuple(acc + _dot(p, v_ref[pl.ds(off, blk), sl]) for acc, p, sl in zip(accs, attn, lanes))
        carries = tuple(cr + c[:, blk:] for cr, c in zip(carries, css))
        top = functools.reduce(jnp.maximum, carries)
        live = (jnp.max(top) > SB_DEAD_LOG).astype(jnp.int32)
        return kb - 1, live, accs, carries

    zeros = tuple(jnp.zeros((blk, blk), F32) for _ in range(heads))
    st = sweep((qb, jnp.int32(1), zeros, zeros), True)
    st = lax.while_loop(cond, functools.partial(sweep, diagonal=False), st)
    for acc, sl in zip(st[2], lanes):
        o_ref[:, sl] = acc.astype(o_ref.dtype)


def _sb_attention(q, k, v):
    t, d = q.shape
    nh = d // SB_HEAD_DIM
    heads = _pick(nh, (8, 4, 2, 1))
    wd = heads * SB_HEAD_DIM
    resident = pl.BlockSpec((t, wd), lambda h, i: (0, h), pipeline_mode=pl.Buffered(1))
    return pl.pallas_call(
        functools.partial(_sb_kernel, heads=heads),
        out_shape=jax.ShapeDtypeStruct((t, d), BF16),
        grid=(nh // heads, t // SB_BLOCK),
        in_specs=[pl.BlockSpec((SB_BLOCK, wd), lambda h, i: (i, h)), resident, resident],
        out_specs=pl.BlockSpec((SB_BLOCK, wd), lambda h, i: (i, h)),
        compiler_params=_cparams("parallel", "arbitrary"),
        name="sb_attention",
    )(q, k, v)


def _glu_up_kernel(x_ref, xh_ref, wg_ref, wu_ref, cw_ref, cb_ref, o_ref):
    i = pl.program_id(0)
    x = x_ref[...]
    wg = wg_ref[...].astype(BF16)
    gate = _dot(x, wg)
    up = _dot(x, wu_ref[...].astype(BF16))
    halo = _dot(xh_ref[...], wg) * (i > 0).astype(F32)
    row = lax.broadcasted_iota(jnp.int32, gate.shape, 0)
    h1 = halo[GLU_HALO - 1:GLU_HALO, :]
    h2 = halo[GLU_HALO - 2:GLU_HALO - 1, :]
    g1 = jnp.where(row == 0, h1, pltpu.roll(gate, 1, 0))
    g2 = jnp.where(row == 0, h2, jnp.where(row == 1, h1, pltpu.roll(gate, 2, 0)))
    cw = cw_ref[...]
    conv = cb_ref[...] + g2 * cw[0:1, :] + g1 * cw[1:2, :] + gate * cw[2:3, :]
    o_ref[...] = (jax.nn.silu(conv) * up).astype(o_ref.dtype)


def _glu_up(x, w_up, conv_w, conv_b, layer):
    t, d = x.shape
    f = w_up.shape[2] // 2
    bm = _pick(t, (2048, 1024, 512, 256, 128, 64, 32, 16))
    bn = _pick(f, (256, 128))
    nf = f // bn
    per = bm // GLU_HALO
    return pl.pallas_call(
        _glu_up_kernel,
        out_shape=jax.ShapeDtypeStruct((t, f), BF16),
        grid=(t // bm, nf),
        in_specs=[pl.BlockSpec((bm, d), lambda i, j: (i, 0), pipeline_mode=pl.Buffered(1)),
                  pl.BlockSpec((GLU_HALO, d), lambda i, j: (jnp.maximum(i * per - 1, 0), 0)),
                  pl.BlockSpec((None, d, bn), lambda i, j: (layer, 0, j)),
                  pl.BlockSpec((None, d, bn), lambda i, j: (layer, 0, j + nf)),
                  pl.BlockSpec((None, CONV_WIDTH, bn), lambda i, j: (layer, 0, j)),
                  pl.BlockSpec((None, 1, bn), lambda i, j: (layer, 0, j))],
        out_specs=pl.BlockSpec((bm, bn), lambda i, j: (i, j)),
        compiler_params=_cparams("parallel", "arbitrary"),
        name="glu_up",
    )(x, x, w_up, w_up, conv_w, conv_b.reshape(conv_b.shape[0], 1, f))


def kernel(x, ln_mix_w, ln_ffn_w, rwkv_mix, rwkv_w_r, rwkv_w_k, rwkv_w_v, rwkv_w_o, rwkv_decay_w0, rwkv_decay_w1, rwkv_decay_w2, rwkv_iclr_a0, rwkv_iclr_a1, rwkv_iclr_a2, rwkv_gate_g1, rwkv_gate_g2, rwkv_k_k, rwkv_k_a, rwkv_r_k, rwkv_lnx_w, rwkv_lnx_b, vres_v0, vres_v1, vres_v2, kv_norm_w, sb_w_k, sb_w_v, sb_w_q, sb_w_o, ffn_w_up, ffn_conv_w, ffn_conv_b, ffn_w_down, final_norm_w):
    bsz, seqlen, d = x.shape
    assert bsz == 1
    depth = ln_mix_w.shape[0]
    n_a = rwkv_mix.shape[0]
    f = ffn_w_down.shape[1]
    h = x.reshape(seqlen, d)
    zero_bias = jnp.zeros((d,), F32)
    sb_k3 = sb_w_k.reshape(1, d, d)
    sb_v3 = sb_w_v.reshape(1, d, d)
    bk_down = f // 2 if (f // 2) % LANES == 0 else f

    v_first = None
    shared_k = shared_v = None
    for layer in range(depth):
        if layer < n_a:
            i = layer
            xr, xw, xk, xv, xa, xg = _norm_mix(h, ln_mix_w[layer], rwkv_mix[i])
            r = _matmul(xr, rwkv_w_r, i, out_dtype=F32)
            k = _matmul(xk, rwkv_w_k, i, out_dtype=F32)
            v = _matmul(xv, rwkv_w_v, i, out_dtype=F32)
            wraw = _lora(xw, rwkv_decay_w1[i], rwkv_decay_w2[i], rwkv_decay_w0[i], mid="tanh")
            a = _lora(xa, rwkv_iclr_a1[i], rwkv_iclr_a2[i], rwkv_iclr_a0[i], post="sigmoid")
            g = _lora(xg, rwkv_gate_g1[i], rwkv_gate_g2[i], zero_bias, mid="sigmoid")
            if i == 0:
                v_first = v
                vgate = None
            else:
                vgate = _lora(xv, vres_v1[i - 1], vres_v2[i - 1], vres_v0[i - 1], post="sigmoid")
            mixed = _wkv(r, k, v, wraw, a, g, v_first if i > 0 else None, vgate,
                         rwkv_k_k[i], rwkv_k_a[i], rwkv_r_k[i], rwkv_lnx_w[i], rwkv_lnx_b[i])
            h = _matmul(mixed, rwkv_w_o, i, out_dtype=F32, res=h, in_place=layer > 0)
        else:
            j = layer - n_a
            if shared_k is None:
                kvn = _rmsnorm(h, kv_norm_w, BF16)
                shared_k = _matmul(kvn, sb_k3, 0, out_dtype=BF16)
                shared_v = _matmul(kvn, sb_v3, 0, out_dtype=BF16)
            hn = _rmsnorm(h, ln_mix_w[layer], BF16)
            q = _matmul(hn, sb_w_q, j, out_dtype=BF16)
            o = _sb_attention(q, shared_k, shared_v)
            h = _matmul(o, sb_w_o, j, out_dtype=F32, res=h)
        hn2 = _rmsnorm(h, ln_ffn_w[layer], BF16)
        mid = _glu_up(hn2, ffn_w_up, ffn_conv_w, ffn_conv_b, layer)
        h = _matmul(mid, ffn_w_down, layer, out_dtype=F32, res=h, bk=bk_down, bn=_pick(d, (256, 128)))
    out = _rmsnorm(h, final_norm_w, F32)
    return out.reshape(bsz, seqlen, d)
```

```python
import functools

import jax
import jax.numpy as jnp
from jax import lax
from jax.experimental import pallas as pl
from jax.experimental.pallas import tpu as pltpu

F32 = jnp.float32
BF16 = jnp.bfloat16

RWKV_HEAD = 64
LNX_EPS = 64e-5
SB_HEAD_DIM = 128
SB_BLOCK = 128
CONV_WIDTH = 3
RMS_EPS = 1e-6

LANES = 128
SUBLANES = 8
GLU_HALO = 2 * SUBLANES
VMEM_LIMIT_BYTES = 56 * 1024 * 1024

WKV_CHUNK = LANES // 2
SB_DEAD_LOG = -110.0


def _cparams(*sem):
    return pltpu.CompilerParams(dimension_semantics=sem, vmem_limit_bytes=VMEM_LIMIT_BYTES)


def _pick(n, prefs):
    for p in prefs:
        if n % p == 0:
            return p
    return n


def _dot(a, b, dims=(((1,), (0,)), ((), ())), precision=None):
    return lax.dot_general(a, b, dims, precision=precision, preferred_element_type=F32)


_NT = (((1,), (1,)), ((), ()))
_TN = (((0,), (0,)), ((), ()))
_HI = lax.Precision.HIGHEST


def _rms(x, w):
    return x * lax.rsqrt(jnp.mean(x * x, axis=-1, keepdims=True) + RMS_EPS) * w


def _rmsnorm_kernel(x_ref, w_ref, o_ref):
    o_ref[...] = _rms(x_ref[...], w_ref[...]).astype(o_ref.dtype)


def _rmsnorm(x, w, out_dtype):
    t, d = x.shape
    bt = _pick(t, (512, 256, 128, 64, 32, 16))
    return pl.pallas_call(
        _rmsnorm_kernel,
        out_shape=jax.ShapeDtypeStruct((t, d), out_dtype),
        grid=(t // bt,),
        in_specs=[pl.BlockSpec((bt, d), lambda i: (i, 0)),
                  pl.BlockSpec((1, d), lambda i: (0, 0))],
        out_specs=pl.BlockSpec((bt, d), lambda i: (i, 0)),
        compiler_params=_cparams("parallel"),
        name="rmsnorm",
    )(x, w.reshape(1, d))


def _norm_mix_kernel(x_ref, xp_ref, w_ref, mix_ref, *o_refs):
    i = pl.program_id(0)
    w = w_ref[...]
    hn = _rms(x_ref[...], w)
    prev = _rms(xp_ref[SUBLANES - 1:SUBLANES, :], w) * (i > 0).astype(F32)
    row = lax.broadcasted_iota(jnp.int32, hn.shape, 0)
    shifted = jnp.where(row == 0, prev, pltpu.roll(hn, 1, 0))
    xx = shifted - hn
    for j, o_ref in enumerate(o_refs):
        o_ref[...] = (hn + xx * mix_ref[j:j + 1, :]).astype(o_ref.dtype)


def _norm_mix(x, w, mix):
    t, d = x.shape
    bt = _pick(t, (256, 128, 64, 32, 16))
    per = bt // SUBLANES
    n = mix.shape[0]
    return pl.pallas_call(
        _norm_mix_kernel,
        out_shape=[jax.ShapeDtypeStruct((t, d), BF16)] * n,
        grid=(t // bt,),
        in_specs=[pl.BlockSpec((bt, d), lambda i: (i, 0)),
                  pl.BlockSpec((SUBLANES, d), lambda i: (jnp.maximum(i * per - 1, 0), 0)),
                  pl.BlockSpec((1, d), lambda i: (0, 0)),
                  pl.BlockSpec((n, d), lambda i: (0, 0))],
        out_specs=[pl.BlockSpec((bt, d), lambda i: (i, 0))] * n,
        compiler_params=_cparams("parallel"),
        name="norm_mix",
    )(x, x, w.reshape(1, d), mix)


def _mm_kernel(*refs, nk, has_res):
    a_ref, w_ref = refs[0], refs[1]
    res_ref = refs[2] if has_res else None
    o_ref = refs[2 + has_res]
    acc_ref = refs[3 + has_res] if nk > 1 else None
    part = _dot(a_ref[...], w_ref[...].astype(BF16))

    def finish(acc):
        if has_res:
            acc = acc + res_ref[...]
        o_ref[...] = acc.astype(o_ref.dtype)

    if nk == 1:
        finish(part)
    else:
        kk = pl.program_id(1)
        j = pl.program_id(2)

        @pl.when(kk == 0)
        def _():
            acc_ref[j] = part

        @pl.when(jnp.logical_and(kk > 0, kk < nk - 1))
        def _():
            acc_ref[j] += part

        @pl.when(kk == nk - 1)
        def _():
            finish(acc_ref[j] + part)


def _matmul(a, w, layer, *, out_dtype, res=None, in_place=True, bm=None, bn=None, bk=None):
    m, k = a.shape
    n = w.shape[2]
    bm = bm or _pick(m, (1024, 512, 256, 128, 64, 32, 16))
    bn = bn or _pick(n, (512, 256, 128))
    bk = bk or k
    nk = k // bk
    nj = n // bn
    assert m % bm == 0 and n % bn == 0 and k % bk == 0

    def out_idx(i, q, j):
        return (i, j) if nk == 1 else (i, jnp.where(q == nk - 1, j, 0))

    a_mode = {} if nk == 1 else {"pipeline_mode": pl.Buffered(1)}
    in_specs = [pl.BlockSpec((bm, bk), lambda i, q, j: (i, q), **a_mode),
                pl.BlockSpec((None, bk, bn), lambda i, q, j: (layer, q, j))]
    args = [a, w]
    aliases = {}
    if res is not None:
        in_specs.append(pl.BlockSpec((bm, bn), out_idx))
        args.append(res)
        aliases = {2: 0} if in_place else {}
    return pl.pallas_call(
        functools.partial(_mm_kernel, nk=nk, has_res=res is not None),
        out_shape=jax.ShapeDtypeStruct((m, n), out_dtype),
        grid=(m // bm, nk, nj),
        in_specs=in_specs,
        out_specs=pl.BlockSpec((bm, bn), out_idx),
        scratch_shapes=[pltpu.VMEM((nj, bm, bn), F32)] if nk > 1 else [],
        input_output_aliases=aliases,
        compiler_params=_cparams("parallel", "arbitrary", "arbitrary"),
        name="matmul",
    )(*args)


def _lora_kernel(x_ref, w1_ref, w2_ref, b_ref, o_ref, *, mid, post):
    hmid = _dot(x_ref[...], w1_ref[...].astype(BF16))
    if mid == "tanh":
        hmid = jnp.tanh(hmid)
    elif mid == "sigmoid":
        hmid = jax.nn.sigmoid(hmid)
    out = _dot(hmid.astype(BF16), w2_ref[...].astype(BF16)) + b_ref[...]
    if post == "sigmoid":
        out = jax.nn.sigmoid(out)
    o_ref[...] = out.astype(o_ref.dtype)


def _lora(x, w1, w2, bias, *, mid=None, post=None):
    t, d = x.shape
    r = w1.shape[1]
    rp = -(-r // LANES) * LANES
    if rp != r:
        w1 = jnp.pad(w1, ((0, 0), (0, rp - r)))
        w2 = jnp.pad(w2, ((0, rp - r), (0, 0)))
    bt = _pick(t, (512, 256, 128, 64, 32, 16))
    return pl.pallas_call(
        functools.partial(_lora_kernel, mid=mid, post=post),
        out_shape=jax.ShapeDtypeStruct((t, d), F32),
        grid=(t // bt,),
        in_specs=[pl.BlockSpec((bt, d), lambda i: (i, 0)),
                  pl.BlockSpec((d, rp), lambda i: (0, 0)),
                  pl.BlockSpec((rp, d), lambda i: (0, 0)),
                  pl.BlockSpec((1, d), lambda i: (0, 0))],
        out_specs=pl.BlockSpec((bt, d), lambda i: (i, 0)),
        compiler_params=_cparams("parallel"),
        name="lora",
    )(x, w1, w2, bias.reshape(1, d))


def _split3(x):
    hi = x.astype(BF16)
    r1 = x - hi.astype(F32)
    mid = r1.astype(BF16)
    lo = (r1 - mid.astype(F32)).astype(BF16)
    return hi, mid, lo


def _dot_01(x, m_bf16):
    hi = x.astype(BF16)
    lo = (x - hi.astype(F32)).astype(BF16)
    return _dot(hi, m_bf16) + _dot(lo, m_bf16)


def _softplus(x):
    return jnp.maximum(x, 0.0) + jnp.log(1.0 + jnp.exp(-jnp.abs(x)))


def _wkv_kernel(*refs, pairs, mix_v):
    if mix_v:
        (r_ref, k_ref, v_ref, wr_ref, a_ref, g_ref, vf_ref, vg_ref,
         kk_ref, ka_ref, rk_ref, lnw_ref, lnb_ref, o_ref, state_ref) = refs
    else:
        (r_ref, k_ref, v_ref, wr_ref, a_ref, g_ref,
         kk_ref, ka_ref, rk_ref, lnw_ref, lnb_ref, o_ref, state_ref) = refs
    c = WKV_CHUNK
    n2 = 2 * c

    @pl.when(pl.program_id(1) == 0)
    def _():
        state_ref[...] = jnp.zeros_like(state_ref)

    row = lax.broadcasted_iota(jnp.int32, (n2, n2), 0)
    col = lax.broadcasted_iota(jnp.int32, (n2, n2), 1)
    same = (row < c) == (col < c)
    strict = jnp.logical_and(same, col < row)
    incl = jnp.logical_and(same, col <= row)
    first_head = lax.broadcasted_iota(jnp.int32, (c, n2), 1) < RWKV_HEAD
    m0b = first_head.astype(BF16)
    m1b = jnp.logical_not(first_head).astype(BF16)
    blk_ones = same.astype(BF16)
    tri = (lax.broadcasted_iota(jnp.int32, (c, c), 0)
           >= lax.broadcasted_iota(jnp.int32, (c, c), 1)).astype(BF16)
    eye = (row == col).astype(F32)

    def unstack(xs):
        return xs[:c] + xs[c:]

    def to_rows(x):
        return jnp.concatenate([x[:, p * LANES:(p + 1) * LANES] for p in range(pairs)], axis=0)

    def to_lanes(x):
        return jnp.concatenate([x[p * c:(p + 1) * c] for p in range(pairs)], axis=1)

    def head_sum(x):
        return to_lanes(_dot_01(to_rows(x), blk_ones))

    r = r_ref[...]
    k = k_ref[...]
    v = v_ref[...]
    a = a_ref[...]
    if mix_v:
        v = v + (vf_ref[...] - v) * vg_ref[...]
    lw = -jnp.exp(-_softplus(-wr_ref[...]) - 0.5)
    kraw = k * kk_ref[...]
    kn = kraw / jnp.maximum(jnp.sqrt(head_sum(kraw * kraw)), 1e-12)
    k2 = k * (1.0 + (a - 1.0) * ka_ref[...])
    lw_hi, lw_mid, lw_lo = _split3(lw)
    cum = _dot(tri, lw_hi) + _dot(tri, lw_mid) + _dot(tri, lw_lo)
    g_in = jnp.exp(cum)
    g_inv = jnp.exp(-cum)
    ah_all = -kn * jnp.exp(cum - lw)
    rh_all = r * g_in
    bh_all = kn * a * g_inv
    kh_all = k2 * g_inv
    bonus = head_sum(r * k2 * rk_ref[...]) * v

    def per_pair(fn, *xs):
        return [fn(*[x[q] for x in xs]) for q in range(pairs)]

    def lanes(x):
        return [x[:, q * LANES:(q + 1) * LANES] for q in range(pairs)]

    def stack2(x, y):
        return jnp.concatenate([x * m0b, x * m1b, y * m0b, y * m1b], axis=0)

    g_last = g_in[c - 1:c, :]
    g_end = lanes(g_last)
    vs = per_pair(lambda x: jnp.concatenate([x * m0b, x * m1b], axis=0), lanes(v.astype(BF16)))
    lhs = per_pair(stack2, lanes(ah_all.astype(BF16)), lanes(rh_all.astype(BF16)))
    rhs = per_pair(stack2, lanes(bh_all.astype(BF16)), lanes(kh_all.astype(BF16)))
    rhs_end = per_pair(stack2, lanes((bh_all * g_last).astype(BF16)),
                       lanes((kh_all * g_last).astype(BF16)))
    gram = per_pair(lambda x, y: _dot(x, y, _NT), lhs, rhs)
    a_ab = [jnp.where(strict, gm[:n2, :n2], 0.0) for gm in gram]
    a_low = [jnp.concatenate([jnp.where(strict, gm[:n2, n2:], 0.0),
                              jnp.where(incl, gm[n2:, n2:], 0.0)], axis=0).astype(BF16) for gm in gram]
    a_rb = [jnp.where(incl, gm[n2:, :n2], 0.0).astype(BF16) for gm in gram]

    rinv = [eye + x for x in a_ab]
    pw = per_pair(lambda x: _dot(x.astype(BF16), x.astype(BF16)), a_ab)
    levels = (c - 1).bit_length()
    for lvl in range(1, levels):
        if lvl < levels - 1:
            both = per_pair(lambda pq, rq: _dot(pq.astype(BF16),
                                                jnp.concatenate([rq, pq], axis=1).astype(BF16)), pw, rinv)
            rinv = [rq + bq[:, :n2] for rq, bq in zip(rinv, both)]
            pw = [bq[:, n2:] for bq in both]
        else:
            rinv = per_pair(lambda pq, rq: rq + _dot(pq.astype(BF16), rq.astype(BF16)), pw, rinv)

    s_t = [state_ref[q] for q in range(pairs)]
    from_state = per_pair(lambda x, s: _dot(x, s.astype(BF16), _NT), lhs, s_t)
    from_v = per_pair(_dot, a_low, vs)
    us = per_pair(lambda rq, fs, fv: _dot(rq.astype(BF16), (fs[:n2] + fv[:n2]).astype(BF16)).astype(BF16),
                  rinv, from_state, from_v)
    ys = per_pair(lambda fs, fv, ab, u: fs[n2:] + fv[n2:] + _dot(ab, u), from_state, from_v, a_rb, us)
    upd = per_pair(lambda u, vq, re: _dot(jnp.concatenate([u, vq], axis=0), re, _TN), us, vs, rhs_end)
    for q in range(pairs):
        state_ref[q] = jnp.where(same, s_t[q] * g_end[q] + upd[q], 0.0)
    y = jnp.concatenate([unstack(x) for x in ys], axis=1)

    inv_n = 1.0 / RWKV_HEAD
    dlt = y - head_sum(y) * inv_n
    var = head_sum(dlt * dlt) * inv_n
    yn = dlt * lax.rsqrt(var + LNX_EPS) * lnw_ref[...] + lnb_ref[...]
    o_ref[...] = ((yn + bonus) * g_ref[...]).astype(o_ref.dtype)


def _wkv(r, k, v, wraw, a, g, vfirst, vgate, k_k, k_a, r_k, lnx_w, lnx_b, *, pairs=None):
    t, d = r.shape
    c = WKV_CHUNK
    npairs = d // LANES
    pairs = pairs or _pick(npairs, (32, 16, 8, 4, 2, 1))
    wd = pairs * LANES
    mix_v = vfirst is not None
    seqs = [r, k, v, wraw, a, g] + ([vfirst, vgate] if mix_v else [])
    seq = pl.BlockSpec((c, wd), lambda h, i: (i, h))
    par = pl.BlockSpec((1, wd), lambda h, i: (0, h))
    prm = [x.reshape(1, d) for x in (k_k, k_a, r_k, lnx_w, lnx_b)]
    return pl.pallas_call(
        functools.partial(_wkv_kernel, pairs=pairs, mix_v=mix_v),
        out_shape=jax.ShapeDtypeStruct((t, d), BF16),
        grid=(npairs // pairs, t // c),
        in_specs=[seq] * len(seqs) + [par] * len(prm),
        out_specs=seq,
        scratch_shapes=[pltpu.VMEM((pairs, LANES, LANES), F32)],
        compiler_params=_cparams("parallel", "arbitrary"),
        name="wkv7",
    )(*seqs, *prm)


def _sb_kernel(q_ref, k_ref, v_ref, o_ref, *, heads):
    blk = SB_BLOCK
    hd = SB_HEAD_DIM
    qb = pl.program_id(1)
    scale = 1.0 / (SB_HEAD_DIM ** 0.5)
    row = lax.broadcasted_iota(jnp.int32, (blk, blk), 0)
    col = lax.broadcasted_iota(jnp.int32, (blk, blk), 1)
    tri = jnp.concatenate([(row > col).astype(BF16), jnp.ones((blk, blk), BF16)], axis=1)
    lanes = [slice(h * hd, (h + 1) * hd) for h in range(heads)]
    qs = [q_ref[:, sl] for sl in lanes]

    def cond(st):
        kb, live = st[0], st[1]
        return jnp.logical_and(kb >= 0, live > 0)

    def sweep(st, diagonal):
        kb, _, accs, carries = st
        off = pl.multiple_of(kb * blk, blk)
        keep = (lambda x: jnp.where(col < row, x, 0.0)) if diagonal else (lambda x: x)
        zs = [_dot(q, k_ref[pl.ds(off, blk), sl], _NT) * scale for q, sl in zip(qs, lanes)]
        sps = [_softplus(z) for z in zs]
        l1m = jnp.concatenate([keep(-sp) for sp in sps], axis=0)
        hi = l1m.astype(BF16)
        lo = (l1m - hi.astype(F32)).astype(BF16)
        cs = _dot(hi, tri) + _dot(lo, tri)
        css = [cs[h * blk:(h + 1) * blk] for h in range(heads)]
        attn = [keep(jnp.exp((z - sp) + c[:, :blk] + cr)).astype(BF16)
                for z, sp, c, cr in zip(zs, sps, css, carries)]
        accs = tuple(acc + _dot(p, v_ref[pl.ds(off, blk), sl]) for acc, p, sl in zip(accs, attn, lanes))
        carries = tuple(cr + c[:, blk:] for cr, c in zip(carries, css))
        top = functools.reduce(jnp.maximum, carries)
        live = (jnp.max(top) > SB_DEAD_LOG).astype(jnp.int32)
        return kb - 1, live, accs, carries

    zeros = tuple(jnp.zeros((blk, blk), F32) for _ in range(heads))
    st = sweep((qb, jnp.int32(1), zeros, zeros), True)
    st = lax.while_loop(cond, functools.partial(sweep, diagonal=False), st)
    for acc, sl in zip(st[2], lanes):
        o_ref[:, sl] = acc.astype(o_ref.dtype)


def _sb_attention(q, k, v):
    t, d = q.shape
    nh = d // SB_HEAD_DIM
    heads = _pick(nh, (8, 4, 2, 1))
    wd = heads * SB_HEAD_DIM
    resident = pl.BlockSpec((t, wd), lambda h, i: (0, h), pipeline_mode=pl.Buffered(1))
    return pl.pallas_call(
        functools.partial(_sb_kernel, heads=heads),
        out_shape=jax.ShapeDtypeStruct((t, d), BF16),
        grid=(nh // heads, t // SB_BLOCK),
        in_specs=[pl.BlockSpec((SB_BLOCK, wd), lambda h, i: (i, h)), resident, resident],
        out_specs=pl.BlockSpec((SB_BLOCK, wd), lambda h, i: (i, h)),
        compiler_params=_cparams("parallel", "arbitrary"),
        name="sb_attention",
    )(q, k, v)


def _glu_up_kernel(x_ref, xh_ref, wg_ref, wu_ref, cw_ref, cb_ref, o_ref):
    i = pl.program_id(0)
    x = x_ref[...]
    wg = wg_ref[...].astype(BF16)
    gate = _dot(x, wg)
    up = _dot(x, wu_ref[...].astype(BF16))
    halo = _dot(xh_ref[...], wg) * (i > 0).astype(F32)
    row = lax.broadcasted_iota(jnp.int32, gate.shape, 0)
    h1 = halo[GLU_HALO - 1:GLU_HALO, :]
    h2 = halo[GLU_HALO - 2:GLU_HALO - 1, :]
    g1 = jnp.where(row == 0, h1, pltpu.roll(gate, 1, 0))
    g2 = jnp.where(row == 0, h2, jnp.where(row == 1, h1, pltpu.roll(gate, 2, 0)))
    cw = cw_ref[...]
    conv = cb_ref[...] + g2 * cw[0:1, :] + g1 * cw[1:2, :] + gate * cw[2:3, :]
    o_ref[...] = (jax.nn.silu(conv) * up).astype(o_ref.dtype)


def _glu_up(x, w_up, conv_w, conv_b, layer):
    t, d = x.shape
    f = w_up.shape[2] // 2
    bm = _pick(t, (2048, 1024, 512, 256, 128, 64, 32, 16))
    bn = _pick(f, (256, 128))
    nf = f // bn
    per = bm // GLU_HALO
    return pl.pallas_call(
        _glu_up_kernel,
        out_shape=jax.ShapeDtypeStruct((t, f), BF16),
        grid=(t // bm, nf),
        in_specs=[pl.BlockSpec((bm, d), lambda i, j: (i, 0), pipeline_mode=pl.Buffered(1)),
                  pl.BlockSpec((GLU_HALO, d), lambda i, j: (jnp.maximum(i * per - 1, 0), 0)),
                  pl.BlockSpec((None, d, bn), lambda i, j: (layer, 0, j)),
                  pl.BlockSpec((None, d, bn), lambda i, j: (layer, 0, j + nf)),
                  pl.BlockSpec((None, CONV_WIDTH, bn), lambda i, j: (layer, 0, j)),
                  pl.BlockSpec((None, 1, bn), lambda i, j: (layer, 0, j))],
        out_specs=pl.BlockSpec((bm, bn), lambda i, j: (i, j)),
        compiler_params=_cparams("parallel", "arbitrary"),
        name="glu_up",
    )(x, x, w_up, w_up, conv_w, conv_b.reshape(conv_b.shape[0], 1, f))


def kernel(x, ln_mix_w, ln_ffn_w, rwkv_mix, rwkv_w_r, rwkv_w_k, rwkv_w_v, rwkv_w_o, rwkv_decay_w0, rwkv_decay_w1, rwkv_decay_w2, rwkv_iclr_a0, rwkv_iclr_a1, rwkv_iclr_a2, rwkv_gate_g1, rwkv_gate_g2, rwkv_k_k, rwkv_k_a, rwkv_r_k, rwkv_lnx_w, rwkv_lnx_b, vres_v0, vres_v1, vres_v2, kv_norm_w, sb_w_k, sb_w_v, sb_w_q, sb_w_o, ffn_w_up, ffn_conv_w, ffn_conv_b, ffn_w_down, final_norm_w):
    bsz, seqlen, d = x.shape
    assert bsz == 1
    depth = ln_mix_w.shape[0]
    n_a = rwkv_mix.shape[0]
    f = ffn_w_down.shape[1]
    h = x.reshape(seqlen, d)
    zero_bias = jnp.zeros((d,), F32)
    sb_k3 = sb_w_k.reshape(1, d, d)
    sb_v3 = sb_w_v.reshape(1, d, d)
    bk_down = f // 2 if (f // 2) % LANES == 0 else f

    v_first = None
    shared_k = shared_v = None
    for layer in range(depth):
        if layer < n_a:
            i = layer
            xr, xw, xk, xv, xa, xg = _norm_mix(h, ln_mix_w[layer], rwkv_mix[i])
            r = _matmul(xr, rwkv_w_r, i, out_dtype=F32)
            k = _matmul(xk, rwkv_w_k, i, out_dtype=F32)
            v = _matmul(xv, rwkv_w_v, i, out_dtype=F32)
            wraw = _lora(xw, rwkv_decay_w1[i], rwkv_decay_w2[i], rwkv_decay_w0[i], mid="tanh")
            a = _lora(xa, rwkv_iclr_a1[i], rwkv_iclr_a2[i], rwkv_iclr_a0[i], post="sigmoid")
            g = _lora(xg, rwkv_gate_g1[i], rwkv_gate_g2[i], zero_bias, mid="sigmoid")
            if i == 0:
                v_first = v
                vgate = None
            else:
                vgate = _lora(xv, vres_v1[i - 1], vres_v2[i - 1], vres_v0[i - 1], post="sigmoid")
            mixed = _wkv(r, k, v, wraw, a, g, v_first if i > 0 else None, vgate,
                         rwkv_k_k[i], rwkv_k_a[i], rwkv_r_k[i], rwkv_lnx_w[i], rwkv_lnx_b[i])
            h = _matmul(mixed, rwkv_w_o, i, out_dtype=F32, res=h, in_place=layer > 0)
        else:
            j = layer - n_a
            if shared_k is None:
                kvn = _rmsnorm(h, kv_norm_w, BF16)
                shared_k = _matmul(kvn, sb_k3, 0, out_dtype=BF16)
                shared_v = _matmul(kvn, sb_v3, 0, out_dtype=BF16)
            hn = _rmsnorm(h, ln_mix_w[layer], BF16)
            q = _matmul(hn, sb_w_q, j, out_dtype=BF16)
            o = _sb_attention(q, shared_k, shared_v)
            h = _matmul(o, sb_w_o, j, out_dtype=F32, res=h)
        hn2 = _rmsnorm(h, ln_ffn_w[layer], BF16)
        mid = _glu_up(hn2, ffn_w_up, ffn_conv_w, ffn_conv_b, layer)
        h = _matmul(mid, ffn_w_down, layer, out_dtype=F32, res=h, bk=bk_down, bn=_pick(d, (256, 128)))
    out = _rmsnorm(h, final_norm_w, F32)
    return out.reshape(bsz, seqlen, d)
```

```python
import functools

import jax
import jax.numpy as jnp
from jax import lax
from jax.experimental import pallas as pl
from jax.experimental.pallas import tpu as pltpu

F32 = jnp.float32
BF16 = jnp.bfloat16

RWKV_HEAD = 64
LNX_EPS = 64e-5
SB_HEAD_DIM = 128
SB_BLOCK = 128
CONV_WIDTH = 3
RMS_EPS = 1e-6

LANES = 128
SUBLANES = 8
GLU_HALO = 2 * SUBLANES
VMEM_LIMIT_BYTES = 56 * 1024 * 1024

WKV_CHUNK = LANES // 2
SB_DEAD_LOG = -110.0


def _cparams(*sem):
    return pltpu.CompilerParams(dimension_semantics=sem, vmem_limit_bytes=VMEM_LIMIT_BYTES)


def _pick(n, prefs):
    for p in prefs:
        if n % p == 0:
            return p
    return n


def _dot(a, b, dims=(((1,), (0,)), ((), ())), precision=None):
    return lax.dot_general(a, b, dims, precision=precision, preferred_element_type=F32)


_NT = (((1,), (1,)), ((), ()))
_TN = (((0,), (0,)), ((), ()))
_HI = lax.Precision.HIGHEST


def _rms(x, w):
    return x * lax.rsqrt(jnp.mean(x * x, axis=-1, keepdims=True) + RMS_EPS) * w


def _rmsnorm_kernel(x_ref, w_ref, o_ref):
    o_ref[...] = _rms(x_ref[...], w_ref[...]).astype(o_ref.dtype)


def _rmsnorm(x, w, out_dtype):
    t, d = x.shape
    bt = _pick(t, (512, 256, 128, 64, 32, 16))
    return pl.pallas_call(
        _rmsnorm_kernel,
        out_shape=jax.ShapeDtypeStruct((t, d), out_dtype),
        grid=(t // bt,),
        in_specs=[pl.BlockSpec((bt, d), lambda i: (i, 0)),
                  pl.BlockSpec((1, d), lambda i: (0, 0))],
        out_specs=pl.BlockSpec((bt, d), lambda i: (i, 0)),
        compiler_params=_cparams("parallel"),
        name="rmsnorm",
    )(x, w.reshape(1, d))


def _norm_mix_kernel(x_ref, xp_ref, w_ref, mix_ref, *o_refs):
    i = pl.program_id(0)
    w = w_ref[...]
    hn = _rms(x_ref[...], w)
    prev = _rms(xp_ref[SUBLANES - 1:SUBLANES, :], w) * (i > 0).astype(F32)
    row = lax.broadcasted_iota(jnp.int32, hn.shape, 0)
    shifted = jnp.where(row == 0, prev, pltpu.roll(hn, 1, 0))
    xx = shifted - hn
    for j, o_ref in enumerate(o_refs):
        o_ref[...] = (hn + xx * mix_ref[j:j + 1, :]).astype(o_ref.dtype)


def _norm_mix(x, w, mix):
    t, d = x.shape
    bt = _pick(t, (256, 128, 64, 32, 16))
    per = bt // SUBLANES
    n = mix.shape[0]
    return pl.pallas_call(
        _norm_mix_kernel,
        out_shape=[jax.ShapeDtypeStruct((t, d), BF16)] * n,
        grid=(t // bt,),
        in_specs=[pl.BlockSpec((bt, d), lambda i: (i, 0)),
                  pl.BlockSpec((SUBLANES, d), lambda i: (jnp.maximum(i * per - 1, 0), 0)),
                  pl.BlockSpec((1, d), lambda i: (0, 0)),
                  pl.BlockSpec((n, d), lambda i: (0, 0))],
        out_specs=[pl.BlockSpec((bt, d), lambda i: (i, 0))] * n,
        compiler_params=_cparams("parallel"),
        name="norm_mix",
    )(x, x, w.reshape(1, d), mix)


def _mm_kernel(*refs, nk, has_res):
    a_ref, w_ref = refs[0], refs[1]
    res_ref = refs[2] if has_res else None
    o_ref = refs[2 + has_res]
    acc_ref = refs[3 + has_res] if nk > 1 else None
    part = _dot(a_ref[...], w_ref[...].astype(BF16))

    def finish(acc):
        if has_res:
            acc = acc + res_ref[...]
        o_ref[...] = acc.astype(o_ref.dtype)

    if nk == 1:
        finish(part)
    else:
        kk = pl.program_id(1)
        j = pl.program_id(2)

        @pl.when(kk == 0)
        def _():
            acc_ref[j] = part

        @pl.when(jnp.logical_and(kk > 0, kk < nk - 1))
        def _():
            acc_ref[j] += part

        @pl.when(kk == nk - 1)
        def _():
            finish(acc_ref[j] + part)


def _matmul(a, w, layer, *, out_dtype, res=None, in_place=True, bm=None, bn=None, bk=None):
    m, k = a.shape
    n = w.shape[2]
    bm = bm or _pick(m, (1024, 512, 256, 128, 64, 32, 16))
    bn = bn or _pick(n, (512, 256, 128))
    bk = bk or k
    nk = k // bk
    nj = n // bn
    assert m % bm == 0 and n % bn == 0 and k % bk == 0

    def out_idx(i, q, j):
        return (i, j) if nk == 1 else (i, jnp.where(q == nk - 1, j, 0))

    a_mode = {} if nk == 1 else {"pipeline_mode": pl.Buffered(1)}
    in_specs = [pl.BlockSpec((bm, bk), lambda i, q, j: (i, q), **a_mode),
                pl.BlockSpec((None, bk, bn), lambda i, q, j: (layer, q, j))]
    args = [a, w]
    aliases = {}
    if res is not None:
        in_specs.append(pl.BlockSpec((bm, bn), out_idx))
        args.append(res)
        aliases = {2: 0} if in_place else {}
    return pl.pallas_call(
        functools.partial(_mm_kernel, nk=nk, has_res=res is not None),
        out_shape=jax.ShapeDtypeStruct((m, n), out_dtype),
        grid=(m // bm, nk, nj),
        in_specs=in_specs,
        out_specs=pl.BlockSpec((bm, bn), out_idx),
        scratch_shapes=[pltpu.VMEM((nj, bm, bn), F32)] if nk > 1 else [],
        input_output_aliases=aliases,
        compiler_params=_cparams("parallel", "arbitrary", "arbitrary"),
        name="matmul",
    )(*args)


def _lora_kernel(x_ref, w1_ref, w2_ref, b_ref, o_ref, *, mid, post):
    hmid = _dot(x_ref[...], w1_ref[...].astype(BF16))
    if mid == "tanh":
        hmid = jnp.tanh(hmid)
    elif mid == "sigmoid":
        hmid = jax.nn.sigmoid(hmid)
    out = _dot(hmid.astype(BF16), w2_ref[...].astype(BF16)) + b_ref[...]
    if post == "sigmoid":
        out = jax.nn.sigmoid(out)
    o_ref[...] = out.astype(o_ref.dtype)


def _lora(x, w1, w2, bias, *, mid=None, post=None):
    t, d = x.shape
    r = w1.shape[1]
    rp = -(-r // LANES) * LANES
    if rp != r:
        w1 = jnp.pad(w1, ((0, 0), (0, rp - r)))
        w2 = jnp.pad(w2, ((0, rp - r), (0, 0)))
    bt = _pick(t, (512, 256, 128, 64, 32, 16))
    return pl.pallas_call(
        functools.partial(_lora_kernel, mid=mid, post=post),
        out_shape=jax.ShapeDtypeStruct((t, d), F32),
        grid=(t // bt,),
        in_specs=[pl.BlockSpec((bt, d), lambda i: (i, 0)),
                  pl.BlockSpec((d, rp), lambda i: (0, 0)),
                  pl.BlockSpec((rp, d), lambda i: (0, 0)),
                  pl.BlockSpec((1, d), lambda i: (0, 0))],
        out_specs=pl.BlockSpec((bt, d), lambda i: (i, 0)),
        compiler_params=_cparams("parallel"),
        name="lora",
    )(x, w1, w2, bias.reshape(1, d))


def _split3(x):
    hi = x.astype(BF16)
    r1 = x - hi.astype(F32)
    mid = r1.astype(BF16)
    lo = (r1 - mid.astype(F32)).astype(BF16)
    return hi, mid, lo


def _softplus(x):
    return jnp.maximum(x, 0.0) + jnp.log(1.0 + jnp.exp(-jnp.abs(x)))


def _wkv_kernel(*refs, pairs, mix_v):
    if mix_v:
        (r_ref, k_ref, v_ref, wr_ref, a_ref, g_ref, vf_ref, vg_ref,
         kk_ref, ka_ref, rk_ref, lnw_ref, lnb_ref, o_ref, state_ref) = refs
    else:
        (r_ref, k_ref, v_ref, wr_ref, a_ref, g_ref,
         kk_ref, ka_ref, rk_ref, lnw_ref, lnb_ref, o_ref, state_ref) = refs
    c = WKV_CHUNK
    n2 = 2 * c

    @pl.when(pl.program_id(1) == 0)
    def _():
        state_ref[...] = jnp.zeros_like(state_ref)

    row = lax.broadcasted_iota(jnp.int32, (n2, n2), 0)
    col = lax.broadcasted_iota(jnp.int32, (n2, n2), 1)
    same = (row < c) == (col < c)
    strict = jnp.logical_and(same, col < row)
    incl = jnp.logical_and(same, col <= row)
    first_head = lax.broadcasted_iota(jnp.int32, (c, n2), 1) < RWKV_HEAD
    m0b = first_head.astype(BF16)
    m1b = jnp.logical_not(first_head).astype(BF16)
    blk_ones = same.astype(BF16)
    tri = (lax.broadcasted_iota(jnp.int32, (c, c), 0)
           >= lax.broadcasted_iota(jnp.int32, (c, c), 1)).astype(BF16)
    eye = (row == col).astype(F32)

    def unstack(xs):
        return xs[:c] + xs[c:]

    def to_rows(x):
        return jnp.concatenate([x[:, p * LANES:(p + 1) * LANES] for p in range(pairs)], axis=0)

    def to_lanes(x):
        return jnp.concatenate([x[p * c:(p + 1) * c] for p in range(pairs)], axis=1)

    def head_sum(x):
        return to_lanes(_dot(to_rows(x).astype(BF16), blk_ones))

    r = r_ref[...]
    k = k_ref[...]
    v = v_ref[...]
    a = a_ref[...]
    if mix_v:
        v = v + (vf_ref[...] - v) * vg_ref[...]
    lw = -jnp.exp(-_softplus(-wr_ref[...]) - 0.5)
    kraw = k * kk_ref[...]
    kn = kraw / jnp.maximum(jnp.sqrt(head_sum(kraw * kraw)), 1e-12)
    k2 = k * (1.0 + (a - 1.0) * ka_ref[...])
    lw_hi, lw_mid, lw_lo = _split3(lw)
    cum = _dot(tri, lw_hi) + _dot(tri, lw_mid) + _dot(tri, lw_lo)
    g_in = jnp.exp(cum)
    g_inv = jnp.exp(-cum)
    ah_all = -kn * jnp.exp(cum - lw)
    rh_all = r * g_in
    bh_all = kn * a * g_inv
    kh_all = k2 * g_inv
    bonus = head_sum(r * k2 * rk_ref[...]) * v

    def per_pair(fn, *xs):
        return [fn(*[x[q] for x in xs]) for q in range(pairs)]

    def lanes(x):
        return [x[:, q * LANES:(q + 1) * LANES] for q in range(pairs)]

    def stack2(x, y):
        return jnp.concatenate([x * m0b, x * m1b, y * m0b, y * m1b], axis=0)

    g_last = g_in[c - 1:c, :]
    g_end = lanes(g_last)
    vs = per_pair(lambda x: jnp.concatenate([x * m0b, x * m1b], axis=0), lanes(v.astype(BF16)))
    lhs = per_pair(stack2, lanes(ah_all.astype(BF16)), lanes(rh_all.astype(BF16)))
    rhs = per_pair(stack2, lanes(bh_all.astype(BF16)), lanes(kh_all.astype(BF16)))
    rhs_end = per_pair(stack2, lanes((bh_all * g_last).astype(BF16)),
                       lanes((kh_all * g_last).astype(BF16)))
    gram = per_pair(lambda x, y: _dot(x, y, _NT), lhs, rhs)
    a_ab = [jnp.where(strict, gm[:n2, :n2], 0.0) for gm in gram]
    a_low = [jnp.concatenate([jnp.where(strict, gm[:n2, n2:], 0.0),
                              jnp.where(incl, gm[n2:, n2:], 0.0)], axis=0).astype(BF16) for gm in gram]
    a_rb = [jnp.where(incl, gm[n2:, :n2], 0.0).astype(BF16) for gm in gram]

    rinv = [eye + x for x in a_ab]
    pw = per_pair(lambda x: _dot(x.astype(BF16), x.astype(BF16)), a_ab)
    levels = (c - 1).bit_length()
    for lvl in range(1, levels):
        if lvl < levels - 1:
            both = per_pair(lambda pq, rq: _dot(pq.astype(BF16),
                                                jnp.concatenate([rq, pq], axis=1).astype(BF16)), pw, rinv)
            rinv = [rq + bq[:, :n2] for rq, bq in zip(rinv, both)]
            pw = [bq[:, n2:] for bq in both]
        else:
            rinv = per_pair(lambda pq, rq: rq + _dot(pq.astype(BF16), rq.astype(BF16)), pw, rinv)

    s_t = [state_ref[q] for q in range(pairs)]
    known = per_pair(lambda x, al, s, vq: _dot(jnp.concatenate([x, al], axis=1),
                                               jnp.concatenate([s.T.astype(BF16), vq], axis=0)),
                     lhs, a_low, s_t, vs)
    us = per_pair(lambda rq, kn_: _dot(rq.astype(BF16), kn_[:n2].astype(BF16)).astype(BF16), rinv, known)
    ys = per_pair(lambda kn_, ab, u: kn_[n2:] + _dot(ab, u), known, a_rb, us)
    upd = per_pair(lambda u, vq, re: _dot(jnp.concatenate([u, vq], axis=0), re, _TN), us, vs, rhs_end)
    for q in range(pairs):
        state_ref[q] = jnp.where(same, s_t[q] * g_end[q] + upd[q], 0.0)
    y = jnp.concatenate([unstack(x) for x in ys], axis=1)

    inv_n = 1.0 / RWKV_HEAD
    dlt = y - head_sum(y) * inv_n
    var = head_sum(dlt * dlt) * inv_n
    yn = dlt * lax.rsqrt(var + LNX_EPS) * lnw_ref[...] + lnb_ref[...]
    o_ref[...] = ((yn + bonus) * g_ref[...]).astype(o_ref.dtype)


def _wkv(r, k, v, wraw, a, g, vfirst, vgate, k_k, k_a, r_k, lnx_w, lnx_b, *, pairs=None):
    t, d = r.shape
    c = WKV_CHUNK
    npairs = d // LANES
    pairs = pairs or _pick(npairs, (32, 16, 8, 4, 2, 1))
    wd = pairs * LANES
    mix_v = vfirst is not None
    seqs = [r, k, v, wraw, a, g] + ([vfirst, vgate] if mix_v else [])
    seq = pl.BlockSpec((c, wd), lambda h, i: (i, h))
    par = pl.BlockSpec((1, wd), lambda h, i: (0, h))
    prm = [x.reshape(1, d) for x in (k_k, k_a, r_k, lnx_w, lnx_b)]
    return pl.pallas_call(
        functools.partial(_wkv_kernel, pairs=pairs, mix_v=mix_v),
        out_shape=jax.ShapeDtypeStruct((t, d), BF16),
        grid=(npairs // pairs, t // c),
        in_specs=[seq] * len(seqs) + [par] * len(prm),
        out_specs=seq,
        scratch_shapes=[pltpu.VMEM((pairs, LANES, LANES), F32)],
        compiler_params=_cparams("parallel", "arbitrary"),
        name="wkv7",
    )(*seqs, *prm)


def _sb_kernel(q_ref, k_ref, v_ref, o_ref, *, heads):
    blk = SB_BLOCK
    hd = SB_HEAD_DIM
    qb = pl.program_id(1)
    scale = 1.0 / (SB_HEAD_DIM ** 0.5)
    row = lax.broadcasted_iota(jnp.int32, (blk, blk), 0)
    col = lax.broadcasted_iota(jnp.int32, (blk, blk), 1)
    tri = jnp.concatenate([(row > col).astype(BF16), jnp.ones((blk, blk), BF16)], axis=1)
    lanes = [slice(h * hd, (h + 1) * hd) for h in range(heads)]
    qs = [q_ref[:, sl] for sl in lanes]

    def cond(st):
        kb, live = st[0], st[1]
        return jnp.logical_and(kb >= 0, live > 0)

    def sweep(st, diagonal):
        kb, _, accs, carries = st
        off = pl.multiple_of(kb * blk, blk)
        keep = (lambda x: jnp.where(col < row, x, 0.0)) if diagonal else (lambda x: x)
        zs = [_dot(q, k_ref[pl.ds(off, blk), sl], _NT) * scale for q, sl in zip(qs, lanes)]
        sps = [_softplus(z) for z in zs]
        l1m = jnp.concatenate([keep(-sp) for sp in sps], axis=0)
        hi = l1m.astype(BF16)
        lo = (l1m - hi.astype(F32)).astype(BF16)
        cs = _dot(hi, tri) + _dot(lo, tri)
        css = [cs[h * blk:(h + 1) * blk] for h in range(heads)]
        attn = [keep(jnp.exp((z - sp) + c[:, :blk] + cr)).astype(BF16)
                for z, sp, c, cr in zip(zs, sps, css, carries)]
        accs = tuple(acc + _dot(p, v_ref[pl.ds(off, blk), sl]) for acc, p, sl in zip(accs, attn, lanes))
        carries = tuple(cr + c[:, blk:] for cr, c in zip(carries, css))
        top = functools.reduce(jnp.maximum, carries)
        live = (jnp.max(top) > SB_DEAD_LOG).astype(jnp.int32)
        return kb - 1, live, accs, carries

    zeros = tuple(jnp.zeros((blk, blk), F32) for _ in range(heads))
    st = sweep((qb, jnp.int32(1), zeros, zeros), True)
    st = lax.while_loop(cond, functools.partial(sweep, diagonal=False), st)
    for acc, sl in zip(st[2], lanes):
        o_ref[:, sl] = acc.astype(o_ref.dtype)


def _sb_attention(q, k, v):
    t, d = q.shape
    nh = d // SB_HEAD_DIM
    heads = _pick(nh, (8, 4, 2, 1))
    wd = heads * SB_HEAD_DIM
    resident = pl.BlockSpec((t, wd), lambda h, i: (0, h), pipeline_mode=pl.Buffered(1))
    return pl.pallas_call(
        functools.partial(_sb_kernel, heads=heads),
        out_shape=jax.ShapeDtypeStruct((t, d), BF16),
        grid=(nh // heads, t // SB_BLOCK),
        in_specs=[pl.BlockSpec((SB_BLOCK, wd), lambda h, i: (i, h)), resident, resident],
        out_specs=pl.BlockSpec((SB_BLOCK, wd), lambda h, i: (i, h)),
        compiler_params=_cparams("parallel", "arbitrary"),
        name="sb_attention",
    )(q, k, v)


def _glu_up_kernel(x_ref, xh_ref, wg_ref, wu_ref, cw_ref, cb_ref, o_ref):
    i = pl.program_id(0)
    x = x_ref[...]
    wg = wg_ref[...].astype(BF16)
    gate = _dot(x, wg)
    up = _dot(x, wu_ref[...].astype(BF16))
    halo = _dot(xh_ref[...], wg) * (i > 0).astype(F32)
    row = lax.broadcasted_iota(jnp.int32, gate.shape, 0)
    h1 = halo[GLU_HALO - 1:GLU_HALO, :]
    h2 = halo[GLU_HALO - 2:GLU_HALO - 1, :]
    g1 = jnp.where(row == 0, h1, pltpu.roll(gate, 1, 0))
    g2 = jnp.where(row == 0, h2, jnp.where(row == 1, h1, pltpu.roll(gate, 2, 0)))
    cw = cw_ref[...]
    conv = cb_ref[...] + g2 * cw[0:1, :] + g1 * cw[1:2, :] + gate * cw[2:3, :]
    o_ref[...] = (jax.nn.silu(conv) * up).astype(o_ref.dtype)


def _glu_up(x, w_up, conv_w, conv_b, layer):
    t, d = x.shape
    f = w_up.shape[2] // 2
    bm = _pick(t, (2048, 1024, 512, 256, 128, 64, 32, 16))
    bn = _pick(f, (256, 128))
    nf = f // bn
    per = bm // GLU_HALO
    return pl.pallas_call(
        _glu_up_kernel,
        out_shape=jax.ShapeDtypeStruct((t, f), BF16),
        grid=(t // bm, nf),
        in_specs=[pl.BlockSpec((bm, d), lambda i, j: (i, 0), pipeline_mode=pl.Buffered(1)),
                  pl.BlockSpec((GLU_HALO, d), lambda i, j: (jnp.maximum(i * per - 1, 0), 0)),
                  pl.BlockSpec((None, d, bn), lambda i, j: (layer, 0, j)),
                  pl.BlockSpec((None, d, bn), lambda i, j: (layer, 0, j + nf)),
                  pl.BlockSpec((None, CONV_WIDTH, bn), lambda i, j: (layer, 0, j)),
                  pl.BlockSpec((None, 1, bn), lambda i, j: (layer, 0, j))],
        out_specs=pl.BlockSpec((bm, bn), lambda i, j: (i, j)),
        compiler_params=_cparams("parallel", "arbitrary"),
        name="glu_up",
    )(x, x, w_up, w_up, conv_w, conv_b.reshape(conv_b.shape[0], 1, f))


def kernel(x, ln_mix_w, ln_ffn_w, rwkv_mix, rwkv_w_r, rwkv_w_k, rwkv_w_v, rwkv_w_o, rwkv_decay_w0, rwkv_decay_w1, rwkv_decay_w2, rwkv_iclr_a0, rwkv_iclr_a1, rwkv_iclr_a2, rwkv_gate_g1, rwkv_gate_g2, rwkv_k_k, rwkv_k_a, rwkv_r_k, rwkv_lnx_w, rwkv_lnx_b, vres_v0, vres_v1, vres_v2, kv_norm_w, sb_w_k, sb_w_v, sb_w_q, sb_w_o, ffn_w_up, ffn_conv_w, ffn_conv_b, ffn_w_down, final_norm_w):
    bsz, seqlen, d = x.shape
    assert bsz == 1
    depth = ln_mix_w.shape[0]
    n_a = rwkv_mix.shape[0]
    f = ffn_w_down.shape[1]
    h = x.reshape(seqlen, d)
    zero_bias = jnp.zeros((d,), F32)
    sb_k3 = sb_w_k.reshape(1, d, d)
    sb_v3 = sb_w_v.reshape(1, d, d)
    bk_down = f // 2 if (f // 2) % LANES == 0 else f

    v_first = None
    shared_k = shared_v = None
    for layer in range(depth):
        if layer < n_a:
            i = layer
            xr, xw, xk, xv, xa, xg = _norm_mix(h, ln_mix_w[layer], rwkv_mix[i])
            r = _matmul(xr, rwkv_w_r, i, out_dtype=F32)
            k = _matmul(xk, rwkv_w_k, i, out_dtype=F32)
            v = _matmul(xv, rwkv_w_v, i, out_dtype=F32)
            wraw = _lora(xw, rwkv_decay_w1[i], rwkv_decay_w2[i], rwkv_decay_w0[i], mid="tanh")
            a = _lora(xa, rwkv_iclr_a1[i], rwkv_iclr_a2[i], rwkv_iclr_a0[i], post="sigmoid")
            g = _lora(xg, rwkv_gate_g1[i], rwkv_gate_g2[i], zero_bias, mid="sigmoid")
            if i == 0:
                v_first = v
                vgate = None
            else:
                vgate = _lora(xv, vres_v1[i - 1], vres_v2[i - 1], vres_v0[i - 1], post="sigmoid")
            mixed = _wkv(r, k, v, wraw, a, g, v_first if i > 0 else None, vgate,
                         rwkv_k_k[i], rwkv_k_a[i], rwkv_r_k[i], rwkv_lnx_w[i], rwkv_lnx_b[i])
            h = _matmul(mixed, rwkv_w_o, i, out_dtype=F32, res=h, in_place=layer > 0)
        else:
            j = layer - n_a
            if shared_k is None:
                kvn = _rmsnorm(h, kv_norm_w, BF16)
                shared_k = _matmul(kvn, sb_k3, 0, out_dtype=BF16)
                shared_v = _matmul(kvn, sb_v3, 0, out_dtype=BF16)
            hn = _rmsnorm(h, ln_mix_w[layer], BF16)
            q = _matmul(hn, sb_w_q, j, out_dtype=BF16)
            o = _sb_attention(q, shared_k, shared_v)
            h = _matmul(o, sb_w_o, j, out_dtype=F32, res=h)
        hn2 = _rmsnorm(h, ln_ffn_w[layer], BF16)
        mid = _glu_up(hn2, ffn_w_up, ffn_conv_w, ffn_conv_b, layer)
        h = _matmul(mid, ffn_w_down, layer, out_dtype=F32, res=h, bk=bk_down, bn=_pick(d, (256, 128)))
    out = _rmsnorm(h, final_norm_w, F32)
    return out.reshape(bsz, seqlen, d)
```
